```python
import math
import jax, jax.numpy as jnp
from jax import lax
import numpy as np

D_MODEL = 4096
BATCH = 4
SEQ = 2048
DEPTH = 2
DEC_BATCH = 128
DEC_SEQ = 8
PAST_LEN = 16384
PAGE_SIZE = 128

SSD_D_INNER = D_MODEL // 2
SSD_HEADDIM = 64
SSD_HEADS = SSD_D_INNER // SSD_HEADDIM
SSD_GROUPS = 4
SSD_HPG = SSD_HEADS // SSD_GROUPS
SSD_STATE = 128
SSD_CONV = 4
SSD_CHUNK = 128
SSD_XBC = SSD_D_INNER + 2 * SSD_GROUPS * SSD_STATE
POOL_WIDTH = D_MODEL // 4
POOL_WINDOWS = (2, 4, 8, 16)
POOL_GROUP = POOL_WIDTH // len(POOL_WINDOWS)
POOL_HIST = max(POOL_WINDOWS) - 1
POOL_OUT_GROUP = D_MODEL // len(POOL_WINDOWS)
GMLP_WIDTH = D_MODEL // 4
GMLP_CHUNK = 128
GMLP_GROUPS = 8
GMLP_GDIM = GMLP_WIDTH // GMLP_GROUPS
CONF_WIDTH = D_MODEL // 4
CONF_CONV = 31
CONF_HIST = CONF_CONV - 1
N_BRANCH = 4
IN_SIZES = (SSD_D_INNER, SSD_XBC, SSD_HEADS, POOL_WIDTH, 2 * GMLP_WIDTH, 2 * CONF_WIDTH, N_BRANCH * D_MODEL)
IN_COLS = sum(IN_SIZES)
PEER_KEYS = 128
PEER_EXPERTS = PEER_KEYS * PEER_KEYS
PEER_HEADS = 8
PEER_TOPK = 16
PEER_QDIM = 256
PEER_HALF = PEER_QDIM // 2
PEER_TOKEN_BLOCK = 64
PLE_DIM = 256
DEEPNORM_ALPHA = (2.0 * DEPTH) ** 0.25
DEEPNORM_BETA = (8.0 * DEPTH) ** -0.25
NORM_EPS = 1e-5

kernel_name = 'hybrid_ssd_pool_gmlp_conformer_peer'


def split_points(sizes):
    pts, acc = [], 0
    for s in sizes[:-1]:
        acc += s
        pts.append(acc)
    return pts


def layer_norm(x, g, b):
    xf = x.astype(jnp.float32)
    mu = jnp.mean(xf, -1, keepdims=True)
    var = jnp.mean(jnp.square(xf - mu), -1, keepdims=True)
    return ((xf - mu) * lax.rsqrt(var + NORM_EPS) * g.astype(jnp.float32) + b.astype(jnp.float32)).astype(x.dtype)


def grouped_rmsnorm(y, g):
    yg = y.reshape(y.shape[:-1] + (SSD_GROUPS, SSD_D_INNER // SSD_GROUPS))
    yg = yg * lax.rsqrt(jnp.mean(jnp.square(yg), -1, keepdims=True) + NORM_EPS)
    return yg.reshape(y.shape) * g.astype(jnp.float32)


def causal_dwconv(x, hist, w, b):
    xc = jnp.concatenate([hist.astype(x.dtype), x], axis=1)
    y = lax.conv_general_dilated(xc, w.astype(x.dtype)[:, None, :], window_strides=(1,), padding='VALID',
                                 dimension_numbers=('NWC', 'WIO', 'NWC'), feature_group_count=x.shape[-1])
    return y + b.astype(x.dtype), xc[:, xc.shape[1] - hist.shape[1]:]


def ssd_chunked(x, dt, a, b_ssm, c_ssm, h0):
    bsz, L = x.shape[:2]
    q = SSD_CHUNK if L % SSD_CHUNK == 0 else L
    c = L // q
    f32 = jnp.float32
    x = x.astype(f32).reshape(bsz, c, q, SSD_GROUPS, SSD_HPG, SSD_HEADDIM)
    dt = dt.reshape(bsz, c, q, SSD_GROUPS, SSD_HPG)
    bm = b_ssm.astype(f32).reshape(bsz, c, q, SSD_GROUPS, SSD_STATE)
    cm = c_ssm.astype(f32).reshape(bsz, c, q, SSD_GROUPS, SSD_STATE)
    a_cum = jnp.cumsum(dt * a.reshape(SSD_GROUPS, SSD_HPG), axis=2)
    a_t = jnp.moveaxis(a_cum, 2, -1)
    causal = jnp.tril(jnp.ones((q, q), bool))
    seg = jnp.exp(jnp.where(causal, a_t[..., :, None] - a_t[..., None, :], -jnp.inf))
    cb = jnp.einsum('bcign,bcjgn->bcgij', cm, bm)
    xdt = x * dt[..., None]
    y_diag = jnp.einsum('bcgrij,bcjgrp->bcigrp', cb[:, :, :, None] * seg, xdt)
    decay_to_end = jnp.exp(a_cum[:, :, -1:] - a_cum)
    chunk_states = jnp.einsum('bcjgn,bcjgrp->bcgrpn', bm, xdt * decay_to_end[..., None])
    chunk_decay = jnp.exp(a_cum[:, :, -1])

    def step(h, inp):
        s, d = inp
        return h * d[..., None, None] + s, h

    h_init = h0.astype(f32).reshape(bsz, SSD_GROUPS, SSD_HPG, SSD_HEADDIM, SSD_STATE)
    h_final, h_prev = lax.scan(step, h_init, (jnp.moveaxis(chunk_states, 1, 0), jnp.moveaxis(chunk_decay, 1, 0)))
    h_prev = jnp.moveaxis(h_prev, 0, 1)
    y_off = jnp.einsum('bcign,bcgrpn->bcigrp', cm, h_prev) * jnp.exp(a_cum)[..., None]
    y = (y_diag + y_off).reshape(bsz, L, SSD_HEADS, SSD_HEADDIM)
    return y, h_final.reshape(bsz, SSD_HEADS, SSD_HEADDIM, SSD_STATE)


def multiscale_pool(u, hist, start):
    L = u.shape[1]
    xc = jnp.concatenate([hist.astype(u.dtype), u], axis=1)
    cs = jnp.pad(jnp.cumsum(xc.astype(jnp.float32), axis=1), ((0, 0), (1, 0), (0, 0)))
    pos = start + jnp.arange(L)
    outs = []
    for gi, w in enumerate(POOL_WINDOWS):
        sl = slice(gi * POOL_GROUP, (gi + 1) * POOL_GROUP)
        hi = cs[:, POOL_HIST + 1:POOL_HIST + 1 + L, sl]
        lo = cs[:, POOL_HIST + 1 - w:POOL_HIST + 1 - w + L, sl]
        cnt = jnp.minimum(w, pos + 1).astype(jnp.float32)[None, :, None]
        outs.append((hi - lo) / cnt)
    mean = jnp.concatenate(outs, axis=-1)
    return mean - u.astype(jnp.float32), xc[:, xc.shape[1] - POOL_HIST:]


def peer(x, w_q, sub_keys, u_tab, v_tab):
    bsz, L, D = x.shape
    T = bsz * L
    t = x.reshape(T, D)
    q = (t @ w_q).astype(jnp.float32).reshape(T, PEER_HEADS, 2, PEER_HALF)
    s = jnp.einsum('thcd,hckd->thck', q, sub_keys.astype(jnp.float32))
    s_top, i_top = lax.top_k(s, PEER_TOPK)
    cand_s = (s_top[:, :, 0, :, None] + s_top[:, :, 1, None, :]).reshape(T, PEER_HEADS, PEER_TOPK * PEER_TOPK)
    cand_i = (i_top[:, :, 0, :, None] * PEER_KEYS + i_top[:, :, 1, None, :]).reshape(T, PEER_HEADS, PEER_TOPK * PEER_TOPK)
    best_s, pos = lax.top_k(cand_s, PEER_TOPK)
    idx = jnp.take_along_axis(cand_i, pos, axis=-1).reshape(T, PEER_HEADS * PEER_TOPK)
    gate = jax.nn.softmax(best_s, axis=-1).reshape(T, PEER_HEADS * PEER_TOPK)
    blk = math.gcd(T, PEER_TOKEN_BLOCK)

    def expert_block(args):
        xb, ib, gb = args
        u = jnp.take(u_tab, ib, axis=0)
        h = jax.nn.gelu(jnp.einsum('tkd,td->tk', u, xb).astype(jnp.float32), approximate=False)
        v = jnp.take(v_tab, ib, axis=0)
        return jnp.einsum('tk,tkd->td', (gb * h).astype(xb.dtype), v)

    out = lax.map(expert_block, (t.reshape(T // blk, blk, D),
                                 idx.reshape(T // blk, blk, PEER_HEADS * PEER_TOPK),
                                 gate.reshape(T // blk, blk, PEER_HEADS * PEER_TOPK)))
    return out.reshape(bsz, L, D)


def decoder_layer(x, p, ssd_conv_hist, ssd_h0, pool_hist, conf_hist, start,
                  w_in, ssd_conv_w, ssd_conv_b, ssd_dt_bias, ssd_a_log, ssd_d, ssd_norm_g, w_ssd_out,
                  pool_scale, w_pool_out,
                  gmlp_ln_g, gmlp_ln_b, gmlp_ws, gmlp_bs, w_gmlp_out,
                  conf_w, conf_b, conf_ln_g, conf_ln_b, w_conf_out,
                  w_o, ln_mix_g, ln_mix_b,
                  peer_w_q, peer_sub_keys, peer_u, peer_v,
                  ple_w_gate, ple_w_proj, ln_ffn_g, ln_ffn_b):
    bsz, L, _ = x.shape
    f32 = jnp.float32
    proj = x @ w_in
    z, xbc, dt_raw, pool_u, gmlp_h, conf_h, gate_h = jnp.split(proj, split_points(IN_SIZES), axis=-1)

    xbc, ssd_conv_new = causal_dwconv(xbc, ssd_conv_hist, ssd_conv_w, ssd_conv_b)
    xbc = jax.nn.silu(xbc)
    xs, b_ssm, c_ssm = jnp.split(xbc, [SSD_D_INNER, SSD_D_INNER + SSD_GROUPS * SSD_STATE], axis=-1)
    dt = jax.nn.softplus(dt_raw.astype(f32) + ssd_dt_bias.astype(f32))
    a = -jnp.exp(ssd_a_log.astype(f32))
    xs_h = xs.reshape(bsz, L, SSD_HEADS, SSD_HEADDIM)
    y, ssd_h_new = ssd_chunked(xs_h, dt, a, b_ssm.reshape(bsz, L, SSD_GROUPS, SSD_STATE),
                               c_ssm.reshape(bsz, L, SSD_GROUPS, SSD_STATE), ssd_h0)
    y = y + xs_h.astype(f32) * ssd_d.astype(f32)[:, None]
    y = grouped_rmsnorm(y.reshape(bsz, L, SSD_D_INNER) * jax.nn.silu(z.astype(f32)), ssd_norm_g)
    branch_a = y.astype(x.dtype) @ w_ssd_out

    pooled, pool_new = multiscale_pool(pool_u, pool_hist, start)
    pooled = (pooled * pool_scale.astype(f32)).astype(x.dtype).reshape(bsz, L, len(POOL_WINDOWS), POOL_GROUP)
    branch_b = jnp.einsum('blgc,gcd->blgd', pooled, w_pool_out).reshape(bsz, L, D_MODEL)

    gmlp_h = jax.nn.gelu(gmlp_h, approximate=False)
    u, v = jnp.split(gmlp_h, 2, axis=-1)
    v = layer_norm(v, gmlp_ln_g, gmlp_ln_b)
    q = GMLP_CHUNK if L % GMLP_CHUNK == 0 else L
    vc = v.reshape(bsz, L // q, q, GMLP_GROUPS, GMLP_GDIM)
    ws = jnp.where(jnp.tril(jnp.ones((q, q), bool)), gmlp_ws[:, :q, :q], 0.0).astype(v.dtype)
    mixed_c = jnp.einsum('gij,bcjgd->bcigd', ws, vc) + gmlp_bs[:, :q].T[:, :, None].astype(v.dtype)
    branch_c = (u * mixed_c.reshape(bsz, L, GMLP_WIDTH)) @ w_gmlp_out

    ca, cg = jnp.split(conf_h, 2, axis=-1)
    glu = ca * jax.nn.sigmoid(cg)
    cconv, conf_new = causal_dwconv(glu, conf_hist, conf_w, conf_b)
    branch_d = jax.nn.silu(layer_norm(cconv, conf_ln_g, conf_ln_b)) @ w_conf_out

    gates = jax.nn.sigmoid(gate_h).reshape(bsz, L, N_BRANCH, D_MODEL)
    branches = jnp.stack([branch_a, branch_b, branch_c, branch_d], axis=2)
    mixed = jnp.sum(gates * branches, axis=2) @ w_o
    x1 = layer_norm(DEEPNORM_ALPHA * x + mixed, ln_mix_g, ln_mix_b)

    ffn = peer(x1, peer_w_q, peer_sub_keys, peer_u, peer_v)
    ple = jax.nn.sigmoid(x1 @ ple_w_gate) * (p @ ple_w_proj)
    x2 = layer_norm(DEEPNORM_ALPHA * x1 + ffn + ple, ln_ffn_g, ln_ffn_b)
    return x2, ssd_conv_new, ssd_h_new.astype(ssd_h0.dtype), pool_new, conf_new, v


def setup_inputs(seed: int = 0) -> dict:
    key = jax.random.key(seed)
    ks = iter(jax.random.split(key, 64))

    def nrm(shape, scale):
        return jax.random.normal(next(ks), shape, jnp.float32) * scale

    beta = DEEPNORM_BETA
    dt0 = jnp.exp(jax.random.uniform(next(ks), (DEPTH, SSD_HEADS), jnp.float32,
                                     minval=math.log(1e-3), maxval=math.log(1e-1)))
    return {
        'x_prompt': nrm((BATCH, SEQ, D_MODEL), 1.0),
        'x_sample': nrm((DEC_BATCH, DEC_SEQ, D_MODEL), 1.0),
        'p_prompt': nrm((DEPTH, BATCH, SEQ, PLE_DIM), 1.0),
        'p_sample': nrm((DEPTH, DEC_BATCH, DEC_SEQ, PLE_DIM), 1.0),
        'state_ssd_conv': nrm((DEPTH, DEC_BATCH, SSD_CONV - 1, SSD_XBC), 1.0),
        'state_ssd_ssm': nrm((DEPTH, DEC_BATCH, SSD_HEADS, SSD_HEADDIM, SSD_STATE), 0.1),
        'state_pool': nrm((DEPTH, DEC_BATCH, POOL_HIST, POOL_WIDTH), 1.0),
        'state_conf': nrm((DEPTH, DEC_BATCH, CONF_HIST, CONF_WIDTH), 0.5),
        'w_in': nrm((DEPTH, D_MODEL, IN_COLS), D_MODEL ** -0.5),
        'ssd_conv_w': nrm((DEPTH, SSD_CONV, SSD_XBC), SSD_CONV ** -0.5),
        'ssd_conv_b': nrm((DEPTH, SSD_XBC), 0.01),
        'ssd_dt_bias': dt0 + jnp.log(-jnp.expm1(-dt0)),
        'ssd_a_log': jnp.log(jax.random.uniform(next(ks), (DEPTH, SSD_HEADS), jnp.float32, minval=1.0, maxval=16.0)),
        'ssd_d': 1.0 + nrm((DEPTH, SSD_HEADS), 0.01),
        'ssd_norm_g': 1.0 + nrm((DEPTH, SSD_D_INNER), 0.01),
        'w_ssd_out': nrm((DEPTH, SSD_D_INNER, D_MODEL), beta * SSD_D_INNER ** -0.5),
        'pool_scale': 1.0 + nrm((DEPTH, POOL_WIDTH), 0.01),
        'w_pool_out': nrm((DEPTH, len(POOL_WINDOWS), POOL_GROUP, POOL_OUT_GROUP), beta * POOL_GROUP ** -0.5),
        'gmlp_ln_g': 1.0 + nrm((DEPTH, GMLP_WIDTH), 0.01),
        'gmlp_ln_b': nrm((DEPTH, GMLP_WIDTH), 0.01),
        'gmlp_ws': nrm((DEPTH, GMLP_GROUPS, GMLP_CHUNK, GMLP_CHUNK), GMLP_CHUNK ** -0.5),
        'gmlp_bs': 1.0 + nrm((DEPTH, GMLP_GROUPS, GMLP_CHUNK), 0.01),
        'w_gmlp_out': nrm((DEPTH, GMLP_WIDTH, D_MODEL), beta * GMLP_WIDTH ** -0.5),
        'conf_w': nrm((DEPTH, CONF_CONV, CONF_WIDTH), CONF_CONV ** -0.5),
        'conf_b': nrm((DEPTH, CONF_WIDTH), 0.01),
        'conf_ln_g': 1.0 + nrm((DEPTH, CONF_WIDTH), 0.01),
        'conf_ln_b': nrm((DEPTH, CONF_WIDTH), 0.01),
        'w_conf_out': nrm((DEPTH, CONF_WIDTH, D_MODEL), beta * CONF_WIDTH ** -0.5),
        'w_o': nrm((DEPTH, D_MODEL, D_MODEL), beta * D_MODEL ** -0.5),
        'ln_mix_g': 1.0 + nrm((DEPTH, D_MODEL), 0.01),
        'ln_mix_b': nrm((DEPTH, D_MODEL), 0.01),
        'peer_w_q': nrm((DEPTH, D_MODEL, PEER_HEADS * PEER_QDIM), D_MODEL ** -0.5),
        'peer_sub_keys': nrm((DEPTH, PEER_HEADS, 2, PEER_KEYS, PEER_HALF), PEER_HALF ** -0.5),
        'peer_u': nrm((DEPTH, PEER_EXPERTS, D_MODEL), D_MODEL ** -0.5),
        'peer_v': nrm((DEPTH, PEER_EXPERTS, D_MODEL), beta),
        'ple_w_gate': nrm((DEPTH, D_MODEL, D_MODEL), D_MODEL ** -0.5),
        'ple_w_proj': nrm((DEPTH, PLE_DIM, D_MODEL), beta * PLE_DIM ** -0.5),
        'ln_ffn_g': 1.0 + nrm((DEPTH, D_MODEL), 0.01),
        'ln_ffn_b': nrm((DEPTH, D_MODEL), 0.01),
    }


def reference(x_prompt, x_sample, p_prompt, p_sample, state_ssd_conv, state_ssd_ssm, state_pool, state_conf,
              w_in, ssd_conv_w, ssd_conv_b, ssd_dt_bias, ssd_a_log, ssd_d, ssd_norm_g, w_ssd_out,
              pool_scale, w_pool_out,
              gmlp_ln_g, gmlp_ln_b, gmlp_ws, gmlp_bs, w_gmlp_out,
              conf_w, conf_b, conf_ln_g, conf_ln_b, w_conf_out,
              w_o, ln_mix_g, ln_mix_b,
              peer_w_q, peer_sub_keys, peer_u, peer_v,
              ple_w_gate, ple_w_proj, ln_ffn_g, ln_ffn_b):
    dt_p = x_prompt.dtype
    yp, ys = x_prompt, x_sample
    conv_p, ssm_p, pool_p, conf_p = [], [], [], []
    conv_s, ssm_s, pool_s, conf_s, gmlp_s = [], [], [], [], []
    for i in range(DEPTH):
        lw = (w_in[i], ssd_conv_w[i], ssd_conv_b[i], ssd_dt_bias[i], ssd_a_log[i], ssd_d[i], ssd_norm_g[i], w_ssd_out[i],
              pool_scale[i], w_pool_out[i],
              gmlp_ln_g[i], gmlp_ln_b[i], gmlp_ws[i], gmlp_bs[i], w_gmlp_out[i],
              conf_w[i], conf_b[i], conf_ln_g[i], conf_ln_b[i], w_conf_out[i],
              w_o[i], ln_mix_g[i], ln_mix_b[i],
              peer_w_q[i], peer_sub_keys[i], peer_u[i], peer_v[i],
              ple_w_gate[i], ple_w_proj[i], ln_ffn_g[i], ln_ffn_b[i])
        yp, cp, hp, pp, fp, _ = decoder_layer(
            yp, p_prompt[i],
            jnp.zeros((BATCH, SSD_CONV - 1, SSD_XBC), dt_p),
            jnp.zeros((BATCH, SSD_HEADS, SSD_HEADDIM, SSD_STATE), dt_p),
            jnp.zeros((BATCH, POOL_HIST, POOL_WIDTH), dt_p),
            jnp.zeros((BATCH, CONF_HIST, CONF_WIDTH), dt_p),
            0, *lw)
        ys, cs, hs, ps, fs, vs = decoder_layer(
            ys, p_sample[i], state_ssd_conv[i], state_ssd_ssm[i], state_pool[i], state_conf[i],
            PAST_LEN, *lw)
        conv_p.append(cp); ssm_p.append(hp); pool_p.append(pp); conf_p.append(fp)
        conv_s.append(cs); ssm_s.append(hs); pool_s.append(ps); conf_s.append(fs); gmlp_s.append(vs)
    new_ssd_conv_prompt = jnp.stack(conv_p)
    new_ssd_ssm_prompt = jnp.stack(ssm_p)
    new_pool_prompt = jnp.stack(pool_p)
    new_conf_prompt = jnp.stack(conf_p)
    new_ssd_conv_sample = jnp.stack(conv_s)
    new_ssd_ssm_sample = jnp.stack(ssm_s)
    new_pool_sample = jnp.stack(pool_s)
    new_conf_sample = jnp.stack(conf_s)
    new_gmlp_v_sample = jnp.stack(gmlp_s)
    return (yp, ys, new_ssd_conv_prompt, new_ssd_ssm_prompt, new_pool_prompt, new_conf_prompt,
            new_ssd_conv_sample, new_ssd_ssm_sample, new_pool_sample, new_conf_sample, new_gmlp_v_sample)
```

```python
import functools
import math

import jax
import jax.numpy as jnp
from jax import lax
from jax.experimental import pallas as pl
from jax.experimental.pallas import tpu as pltpu

f32 = jnp.float32
bf16 = jnp.bfloat16

V7X_VMEM_BYTES = 64 * 1024 * 1024
VMEM_LIMIT = V7X_VMEM_BYTES - 8 * 1024 * 1024
LANES = 128
SUBLANES = 8

NORM_EPS = 1e-5
SSD_HEADDIM = 64
SSD_GROUPS = 4
SSD_STATE = 128
SSD_CONV = 4
POOL_WINDOWS = (2, 4, 8, 16)
GMLP_GROUPS = 8
CONF_CONV = 31
N_BRANCH = 4
PEER_KEYS = 128
PEER_HEADS = 8
PEER_TOPK = 16
ROWS = 128


def _params(*sem):
    return pltpu.CompilerParams(dimension_semantics=sem, vmem_limit_bytes=VMEM_LIMIT)


def _pick(n, prefs):
    for p in prefs:
        if n % p == 0:
            return p
    return n


def _sigmoid(x):
    return 1.0 / (1.0 + jnp.exp(-x))


def _silu(x):
    return x * _sigmoid(x)


def _gelu(x):
    return 0.5 * x * (1.0 + lax.erf(x * (1.0 / math.sqrt(2.0))))


def _softplus(x):
    return jnp.maximum(x, 0.0) + jnp.log(1.0 + jnp.exp(-jnp.abs(x)))


def _mm_kernel(*refs, n_a, w_src, n_extra, epi):
    a_refs = refs[:n_a]
    w_refs = refs[n_a:n_a + len(w_src)]
    e_refs = refs[n_a + len(w_src):n_a + len(w_src) + n_extra]
    o_ref = refs[-1]
    a_vals = [r[...] for r in a_refs]
    accs = [jnp.dot(a_vals[s], w[...], preferred_element_type=f32) for s, w in zip(w_src, w_refs)]
    o_ref[...] = epi(*accs, *[e[...] for e in e_refs]).astype(o_ref.dtype)


def _mm(a_list, w_list, n_out, epi, extras=(), out_dtype=f32, tm=None, tn=None):
    m = a_list[0].shape[0]
    tm = tm or _pick(m, (1024, 512, 256, 128))
    tn = tn or _pick(n_out, (512, 256, 128))
    in_specs, args = [], []
    for a in a_list:
        in_specs.append(pl.BlockSpec((tm, a.shape[1]), lambda j, i: (i, 0)))
        args.append(a)
    for (_, w, off) in w_list:
        ob = off // tn
        in_specs.append(pl.BlockSpec((w.shape[0], tn), lambda j, i, ob=ob: (0, j + ob)))
        args.append(w)
    for (e, kind, off) in extras:
        ob = off // tn
        if kind == 'tile':
            in_specs.append(pl.BlockSpec((tm, tn), lambda j, i, ob=ob: (i, j + ob)))
        else:
            in_specs.append(pl.BlockSpec((1, tn), lambda j, i, ob=ob: (0, j + ob)))
        args.append(e)
    kern = functools.partial(_mm_kernel, n_a=len(a_list), w_src=tuple(s for s, _, _ in w_list),
                             n_extra=len(extras), epi=epi)
    return pl.pallas_call(
        kern,
        out_shape=jax.ShapeDtypeStruct((m, n_out), out_dtype),
        grid=(n_out // tn, m // tm),
        in_specs=in_specs,
        out_specs=pl.BlockSpec((tm, tn), lambda j, i: (i, j)),
        compiler_params=_params("parallel", "arbitrary"),
    )(*args)


def _ln_kernel(x_ref, g_ref, b_ref, *o_refs, act):
    x = x_ref[...]
    mu = jnp.mean(x, axis=-1, keepdims=True)
    xc = x - mu
    var = jnp.mean(xc * xc, axis=-1, keepdims=True)
    y = xc * lax.rsqrt(var + NORM_EPS) * g_ref[...] + b_ref[...]
    if act == 'silu':
        y = _silu(y)
    for o in o_refs:
        o[...] = y.astype(o.dtype)


def _layer_norm(x, g, b, out_dtypes, act=None):
    m, n = x.shape
    tm = _pick(m, (256, 128))
    outs = pl.pallas_call(
        functools.partial(_ln_kernel, act=act),
        out_shape=[jax.ShapeDtypeStruct((m, n), d) for d in out_dtypes],
        grid=(m // tm,),
        in_specs=[pl.BlockSpec((tm, n), lambda i: (i, 0)),
                  pl.BlockSpec((1, n), lambda i: (0, 0)),
                  pl.BlockSpec((1, n), lambda i: (0, 0))],
        out_specs=[pl.BlockSpec((tm, n), lambda i: (i, 0)) for _ in out_dtypes],
        compiler_params=_params("parallel"),
    )(x, g.reshape(1, n), b.reshape(1, n))
    return outs


def _hist_pad(h):
    return -(-h // SUBLANES) * SUBLANES


def _fill_window(xc_ref, x_ref, h_ref, s, seq_len, hist):
    hp = _hist_pad(hist)
    r0 = pl.multiple_of(s * seq_len, SUBLANES)
    xc_ref[pl.ds(hp - hist, hist), :] = h_ref[s]
    xc_ref[pl.ds(hp, seq_len), :] = x_ref[pl.ds(r0, seq_len), :]
    return r0


def _conv_kernel(x_ref, h_ref, w_ref, b_ref, o_ref, xc_ref, *, seq_len, width, n_seq, rt, act):
    hist = width - 1
    hp = _hist_pad(hist)
    w = w_ref[...]
    bias = b_ref[...]

    def one_seq(s, carry):
        r0 = _fill_window(xc_ref, x_ref, h_ref, s, seq_len, hist)

        def tile(t, c):
            base = pl.multiple_of(t * rt, SUBLANES)
            win = xc_ref[pl.ds(base, rt + hp), :]
            acc = jnp.broadcast_to(bias, (rt, bias.shape[1]))
            for k in range(width):
                off = hp - hist + k
                acc = acc + win[off:off + rt] * w[k:k + 1]
            if act == 'silu':
                acc = _silu(acc)
            o_ref[pl.ds(pl.multiple_of(r0 + base, SUBLANES), rt), :] = acc
            return c

        lax.fori_loop(0, seq_len // rt, tile, 0)
        return carry

    lax.fori_loop(0, n_seq, one_seq, 0)


def _seq_call(kernel_fn, x, hist, small, n_batch, seq_len, hist_len, out_dtype, ct):
    c = x.shape[1]
    n_seq = 1 if seq_len >= ROWS else _pick(n_batch, (16, 8, 4, 2, 1))
    rows = n_seq * seq_len
    hp = _hist_pad(hist_len)
    in_specs = [pl.BlockSpec((rows, ct), lambda b, j: (b, j)),
                pl.BlockSpec((n_seq, hist_len, ct), lambda b, j: (b, 0, j))]
    in_specs += [pl.BlockSpec((p.shape[0], ct), lambda b, j: (0, j)) for p in small]
    return pl.pallas_call(
        functools.partial(kernel_fn, seq_len=seq_len, n_seq=n_seq),
        out_shape=jax.ShapeDtypeStruct(x.shape, out_dtype),
        grid=(n_batch // n_seq, c // ct),
        in_specs=in_specs,
        out_specs=pl.BlockSpec((rows, ct), lambda b, j: (b, j)),
        scratch_shapes=[pltpu.VMEM((hp + seq_len, ct), f32)],
        compiler_params=_params("parallel", "parallel"),
    )(x, hist, *small)


def _causal_conv(x, hist, w, b, n_batch, seq_len, act):
    width = w.shape[0]
    rt = _pick(seq_len, (64, 32, 16, 8))
    kern = functools.partial(_conv_kernel, width=width, rt=rt, act=act)
    return _seq_call(kern, x, hist, [w, b.reshape(1, -1)], n_batch, seq_len, width - 1, f32, ct=256)


def _pool_kernel(x_ref, h_ref, s_ref, o_ref, xc_ref, *, seq_len, n_seq, rt, window, hist, first_pos):
    hp = _hist_pad(hist)
    scale = s_ref[...]

    def one_seq(s, carry):
        r0 = _fill_window(xc_ref, x_ref, h_ref, s, seq_len, hist)

        def tile(t, c):
            base = pl.multiple_of(t * rt, SUBLANES)
            win = xc_ref[pl.ds(base, rt + hp), :]
            tok = win[hp:hp + rt]
            acc = tok
            for k in range(1, window):
                acc = acc + win[hp - k:hp - k + rt]
            pos = first_pos + base + lax.broadcasted_iota(jnp.int32, tok.shape, 0)
            cnt = jnp.minimum(window, pos + 1).astype(f32)
            o_ref[pl.ds(pl.multiple_of(r0 + base, SUBLANES), rt), :] = ((acc / cnt - tok) * scale).astype(o_ref.dtype)
            return c

        lax.fori_loop(0, seq_len // rt, tile, 0)
        return carry

    lax.fori_loop(0, n_seq, one_seq, 0)


def _multiscale_pool(u, hist, scale, n_batch, seq_len, first_pos):
    gw = u.shape[1] // len(POOL_WINDOWS)
    rt = _pick(seq_len, (64, 32, 16, 8))
    hist_len = hist.shape[1]
    outs = []
    for gi, window in enumerate(POOL_WINDOWS):
        sl = slice(gi * gw, (gi + 1) * gw)
        kern = functools.partial(_pool_kernel, rt=rt, window=window, hist=hist_len, first_pos=first_pos)
        outs.append(_seq_call(kern, u[:, sl], hist[:, :, sl], [scale[sl].reshape(1, -1)], n_batch, seq_len,
                              hist_len, bf16, ct=gw))
    return outs


def _seg_scan(x, seg_len, reverse):
    n = x.shape[0]
    pos = lax.broadcasted_iota(jnp.int32, x.shape, 0) % seg_len
    s = 1
    while s < seg_len:
        if reverse:
            x = x + jnp.where(pos < seg_len - s, pltpu.roll(x, n - s, axis=0), 0.0)
        else:
            x = x + jnp.where(pos >= s, pltpu.roll(x, s, axis=0), 0.0)
        s *= 2
    return x


def _ssd_kernel(xs_ref, b_ref, c_ref, dt_ref, z_ref, alog_ref, alogx_ref, dx_ref, g_ref, *rest,
                seq_len, n_seq, carry_state, has_h0):
    if has_h0:
        h0_ref, y_ref, hout_ref, h_scr = rest
    else:
        y_ref, hout_ref, h_scr = rest
    grp = pl.program_id(1)
    hpg = xs_ref.shape[1] // SSD_HEADDIM
    width = xs_ref.shape[1]
    rows = xs_ref.shape[0]
    high = lax.Precision.HIGHEST

    xs = xs_ref[...]
    bm = b_ref[...].astype(bf16)
    cm = c_ref[...].astype(bf16)
    dt = dt_ref[...]

    da = dt * (-jnp.exp(alog_ref[...]))
    a_cum = _seg_scan(da, seq_len, reverse=False)
    head_of_col = lax.broadcasted_iota(jnp.int32, (LANES, width), 1) // SSD_HEADDIM + grp * hpg
    expand = (lax.broadcasted_iota(jnp.int32, (LANES, width), 0) == head_of_col).astype(f32)
    dtx = jnp.dot(dt, expand, precision=high, preferred_element_type=f32)
    dax = dtx * (-jnp.exp(alogx_ref[...]))
    a_cum_x = _seg_scan(dax, seq_len, reverse=False)
    to_end_x = _seg_scan(dax, seq_len, reverse=True) - dax

    xdt = xs * dtx
    xdt_b = xdt.astype(bf16)

    if carry_state:
        @pl.when(pl.program_id(2) == 0)
        def _():
            h_scr[...] = jnp.zeros_like(h_scr)
        h_prev = h_scr[...]
    else:
        h_prev = h0_ref[...].reshape(n_seq * width, SSD_STATE)

    cb = lax.dot_general(cm, bm, (((1,), (1,)), ((), ())), preferred_element_type=f32)
    ri = lax.broadcasted_iota(jnp.int32, (rows, rows), 0)
    ci = lax.broadcasted_iota(jnp.int32, (rows, rows), 1)
    allowed = (ri >= ci) & ((ri // seq_len) == (ci // seq_len))
    a_cum_t = a_cum.T
    col_head = lax.broadcasted_iota(jnp.int32, (rows, width), 1) // SSD_HEADDIM
    y = jnp.zeros((rows, width), f32)
    for r in range(hpg):
        hc = grp * hpg + r
        sel = (lax.broadcasted_iota(jnp.int32, (1, LANES), 1) == hc).astype(f32)
        a_col = jnp.sum(a_cum * sel, axis=1, keepdims=True)
        a_row = jnp.sum(a_cum_t * sel.T, axis=0, keepdims=True)
        seg = jnp.exp(jnp.where(allowed, a_col - a_row, -jnp.inf))
        m_r = (cb * seg).astype(bf16)
        y_r = jnp.dot(m_r, xdt_b, preferred_element_type=f32)
        y = y + jnp.where(col_head == r, y_r, 0.0)

    y_all = lax.dot_general(cm, h_prev.astype(bf16), (((1,), (1,)), ((), ())), preferred_element_type=f32)
    xdt_end = (xdt * jnp.exp(to_end_x)).astype(bf16)
    total_x = a_cum_x + to_end_x
    row_seq = lax.broadcasted_iota(jnp.int32, (rows, 1), 0) // seq_len
    ones = jnp.ones((rows, SSD_STATE), f32)
    y_off = jnp.zeros((rows, width), f32)
    for s in range(n_seq):
        in_seq = row_seq == s
        y_off = y_off + jnp.where(in_seq, y_all[:, s * width:(s + 1) * width], 0.0)
        st = lax.dot_general(jnp.where(in_seq, xdt_end, jnp.zeros_like(xdt_end)), bm,
                             (((0,), (0,)), ((), ())), preferred_element_type=f32)
        last = lax.broadcasted_iota(jnp.int32, (rows, 1), 0) == (s * seq_len + seq_len - 1)
        decay = lax.dot_general(jnp.where(last, jnp.exp(total_x), 0.0), ones, (((0,), (0,)), ((), ())),
                                precision=high, preferred_element_type=f32)
        h_new = h_prev[s * width:(s + 1) * width] * decay + st
        if carry_state:
            h_scr[...] = h_new
            hout_ref[...] = h_new.reshape(hout_ref.shape)
        else:
            hout_ref[s] = h_new.reshape(hout_ref.shape[1:])
    y = y + y_off * jnp.exp(a_cum_x)

    y = (y + xs * dx_ref[...]) * z_ref[...]
    y = y * lax.rsqrt(jnp.mean(y * y, axis=-1, keepdims=True) + NORM_EPS) * g_ref[...]
    y_ref[...] = y.astype(y_ref.dtype)


def _ssd(xbc, dt, z_act, a_log, d_skip, norm_g, h0, n_batch, seq_len):
    t = xbc.shape[0]
    n_heads = a_log.shape[0]
    inner = n_heads * SSD_HEADDIM
    width = inner // SSD_GROUPS
    hpg = n_heads // SSD_GROUPS
    carry = seq_len >= ROWS
    q = ROWS if carry else seq_len
    n_seq = 1 if carry else ROWS // seq_len
    n_chunks = seq_len // q if carry else 1
    n_bblk = n_batch if carry else n_batch // n_seq
    alog_pad = jnp.zeros((1, LANES), f32).at[0, :n_heads].set(a_log)
    rep = lambda v: jnp.repeat(v, SSD_HEADDIM).reshape(1, inner)
    b_blk0 = inner // SSD_STATE
    c_blk0 = b_blk0 + SSD_GROUPS
    row = lambda b, g, c: b * n_chunks + c
    in_specs = [
        pl.BlockSpec((ROWS, width), lambda b, g, c: (row(b, g, c), g)),
        pl.BlockSpec((ROWS, SSD_STATE), lambda b, g, c: (row(b, g, c), b_blk0 + g)),
        pl.BlockSpec((ROWS, SSD_STATE), lambda b, g, c: (row(b, g, c), c_blk0 + g)),
        pl.BlockSpec((ROWS, LANES), lambda b, g, c: (row(b, g, c), 0)),
        pl.BlockSpec((ROWS, width), lambda b, g, c: (row(b, g, c), g)),
        pl.BlockSpec((1, LANES), lambda b, g, c: (0, 0)),
        pl.BlockSpec((1, width), lambda b, g, c: (0, g)),
        pl.BlockSpec((1, width), lambda b, g, c: (0, g)),
        pl.BlockSpec((1, width), lambda b, g, c: (0, g)),
    ]
    args = [xbc, xbc, xbc, dt, z_act, alog_pad, rep(a_log), rep(d_skip), norm_g.reshape(1, inner)]
    if not carry:
        in_specs.append(pl.BlockSpec((n_seq, hpg, SSD_HEADDIM, SSD_STATE), lambda b, g, c: (b, g, 0, 0)))
        args.append(h0)
    kern = functools.partial(_ssd_kernel, seq_len=q, n_seq=n_seq, carry_state=carry, has_h0=not carry)
    y, h_final = pl.pallas_call(
        kern,
        out_shape=[jax.ShapeDtypeStruct((t, inner), bf16),
                   jax.ShapeDtypeStruct((n_batch, n_heads, SSD_HEADDIM, SSD_STATE), f32)],
        grid=(n_bblk, SSD_GROUPS, n_chunks),
        in_specs=in_specs,
        out_specs=[pl.BlockSpec((ROWS, width), lambda b, g, c: (row(b, g, c), g)),
                   pl.BlockSpec((n_seq, hpg, SSD_HEADDIM, SSD_STATE), lambda b, g, c: (b, g, 0, 0))],
        scratch_shapes=[pltpu.VMEM((width, SSD_STATE), f32)],
        compiler_params=_params("parallel", "parallel", "arbitrary"),
    )(*args)
    return y, h_final


def _gmlp_kernel(h_ref, g_ref, b_ref, ws_ref, bias_ref, o_ref, v_ref):
    half = h_ref.shape[1] // 2
    u = h_ref[:, :half]
    v = h_ref[:, half:]
    mu = jnp.mean(v, axis=-1, keepdims=True)
    vc = v - mu
    var = jnp.mean(vc * vc, axis=-1, keepdims=True)
    v = vc * lax.rsqrt(var + NORM_EPS) * g_ref[...] + b_ref[...]
    v_ref[...] = v
    vb = v.astype(bf16)
    gd = half // GMLP_GROUPS
    for g in range(GMLP_GROUPS):
        cs = slice(g * gd, (g + 1) * gd)
        mixed = jnp.dot(ws_ref[g], vb[:, cs], preferred_element_type=f32) + bias_ref[:, cs]
        o_ref[:, cs] = (u[:, cs] * mixed).astype(o_ref.dtype)


def _gmlp(h, ln_g, ln_b, ws, bs, seq_len):
    t, two_w = h.shape
    half = two_w // 2
    q = ROWS if seq_len % ROWS == 0 else seq_len
    reps = ROWS // q
    tri = jnp.tril(jnp.ones((q, q), f32))
    ws_q = ws[:, :q, :q] * tri
    ws_blk = jnp.einsum('ab,gij->gaibj', jnp.eye(reps, dtype=f32), ws_q).reshape(GMLP_GROUPS, ROWS, ROWS).astype(bf16)
    bias = jnp.repeat(jnp.tile(bs[:, :q], (1, reps)).T, half // GMLP_GROUPS, axis=1)
    out, v = pl.pallas_call(
        _gmlp_kernel,
        out_shape=[jax.ShapeDtypeStruct((t, half), bf16), jax.ShapeDtypeStruct((t, half), f32)],
        grid=(t // ROWS,),
        in_specs=[pl.BlockSpec((ROWS, two_w), lambda i: (i, 0)),
                  pl.BlockSpec((1, half), lambda i: (0, 0)),
                  pl.BlockSpec((1, half), lambda i: (0, 0)),
                  pl.BlockSpec((GMLP_GROUPS, ROWS, ROWS), lambda i: (0, 0, 0)),
                  pl.BlockSpec((ROWS, half), lambda i: (0, 0))],
        out_specs=[pl.BlockSpec((ROWS, half), lambda i: (i, 0)), pl.BlockSpec((ROWS, half), lambda i: (i, 0))],
        compiler_params=_params("parallel"),
    )(h, ln_g.reshape(1, half), ln_b.reshape(1, half), ws_blk, bias)
    return out, v


def _merge_kernel(a0, a1, a2, a3, w0, w1, w2, w3, g0, g1, g2, g3, o_ref):
    acc = g0[...] * jnp.dot(a0[...], w0[...], preferred_element_type=f32)
    acc += g1[...] * jnp.dot(a1[...], w1[0], preferred_element_type=f32)
    acc += g2[...] * jnp.dot(a2[...], w2[...], preferred_element_type=f32)
    acc += g3[...] * jnp.dot(a3[...], w3[...], preferred_element_type=f32)
    o_ref[...] = acc.astype(o_ref.dtype)


def _merge(a_ssd, a_pool, a_gmlp, a_conf, w_ssd, w_pool, w_gmlp, w_conf, gates, d_model):
    t = a_ssd.shape[0]
    tm = _pick(t, (512, 256, 128))
    tn = 512
    pool_out = w_pool.shape[2]
    per_grp = pool_out // tn
    nb = d_model // tn
    a_pool_cat = jnp.stack(a_pool, axis=0)
    full = lambda a: pl.BlockSpec((tm, a.shape[1]), lambda j, i: (i, 0))
    wcol = lambda w: pl.BlockSpec((w.shape[0], tn), lambda j, i: (0, j))
    gate = lambda k: pl.BlockSpec((tm, tn), lambda j, i, k=k: (i, k * nb + j))
    return pl.pallas_call(
        _merge_kernel,
        out_shape=jax.ShapeDtypeStruct((t, d_model), bf16),
        grid=(nb, t // tm),
        in_specs=[full(a_ssd),
                  pl.BlockSpec((None, tm, a_pool_cat.shape[2]), lambda j, i: (j // per_grp, i, 0)),
                  full(a_gmlp), full(a_conf),
                  wcol(w_ssd),
                  pl.BlockSpec((1, w_pool.shape[1], tn), lambda j, i: (j // per_grp, 0, j % per_grp)),
                  wcol(w_gmlp), wcol(w_conf),
                  gate(0), gate(1), gate(2), gate(3)],
        out_specs=pl.BlockSpec((tm, tn), lambda j, i: (i, j)),
        compiler_params=_params("parallel", "arbitrary"),
    )(a_ssd, a_pool_cat, a_gmlp, a_conf, w_ssd, w_pool, w_gmlp, w_conf, gates, gates, gates, gates)


def _top_values(work, out_ref, count):
    for r in range(count):
        m = jnp.max(work, axis=0, keepdims=True)
        out_ref[pl.ds(r, 1), :] = m
        work = jnp.where(work == m, -jnp.inf, work)


def _peer_select_kernel(q_ref, k_ref, s1_ref, s2_ref, a_ref, b_ref, tau_ref, top1, top2, cand):
    k = PEER_TOPK
    for h in range(PEER_HEADS):
        halves = []
        for c in range(2):
            qb = q_ref[:, (2 * h + c) * LANES:(2 * h + c + 1) * LANES]
            halves.append(lax.dot_general(k_ref[h, c], qb, (((1,), (1,)), ((), ())), preferred_element_type=f32))
        s1, s2 = halves
        _top_values(s1, top1, k)
        _top_values(s2, top2, k)
        t2 = top2[...]
        for i in range(k):
            cand[pl.ds(i * k, k), :] = top1[pl.ds(i, 1), :] + t2
        cv = cand[...]
        work = cv
        for r in range(k):
            tau = jnp.max(work, axis=0, keepdims=True)
            work = jnp.where(work == tau, -jnp.inf, work)
        m1 = top1[pl.ds(0, 1), :]
        m2 = top2[pl.ds(0, 1), :]
        z = jnp.sum(jnp.where(cv >= tau, jnp.exp(cv - (m1 + m2)), 0.0), axis=0, keepdims=True)
        s1_ref[h] = s1
        s2_ref[h] = s2
        a_ref[h] = jnp.exp(s1 - m1) / z
        b_ref[h] = jnp.exp(s2 - m2)
        tau_ref[pl.ds(h, 1), :] = tau


def _peer_select(q, keys):
    t = q.shape[0]
    tt = _pick(t, (256, 128))
    big = jax.ShapeDtypeStruct((PEER_HEADS, PEER_KEYS, t), f32)
    blk = pl.BlockSpec((PEER_HEADS, PEER_KEYS, tt), lambda i: (0, 0, i))
    return pl.pallas_call(
        _peer_select_kernel,
        out_shape=[big, big, big, big, jax.ShapeDtypeStruct((PEER_HEADS, t), f32)],
        grid=(t // tt,),
        in_specs=[pl.BlockSpec((tt, q.shape[1]), lambda i: (i, 0)),
                  pl.BlockSpec(keys.shape, lambda i: (0, 0, 0, 0))],
        out_specs=[blk, blk, blk, blk, pl.BlockSpec((PEER_HEADS, tt), lambda i: (0, i))],
        scratch_shapes=[pltpu.VMEM((PEER_TOPK, tt), f32), pltpu.VMEM((PEER_TOPK, tt), f32),
                        pltpu.VMEM((PEER_TOPK * PEER_TOPK, tt), f32)],
        compiler_params=_params("parallel"),
    )(q, keys)


def _peer_mix_kernel(x_ref, u_ref, v_ref, s1_ref, s2_ref, a_ref, b_ref, tau_ref, o_ref, hw_ref, *, et):
    j = pl.program_id(1)

    @pl.when(j == 0)
    def _():
        o_ref[...] = jnp.zeros_like(o_ref)

    ht = lax.dot_general(u_ref[...], x_ref[...], (((1,), (1,)), ((), ())), preferred_element_type=f32)
    act = _gelu(ht)
    for e in range(et // PEER_KEYS):
        e1 = j * (et // PEER_KEYS) + e
        w = jnp.zeros((PEER_KEYS, ht.shape[1]), f32)
        for h in range(PEER_HEADS):
            s1 = s1_ref[h, pl.ds(e1, 1), :]
            a1 = a_ref[h, pl.ds(e1, 1), :]
            tau = tau_ref[pl.ds(h, 1), :]
            w = w + jnp.where(s1 + s2_ref[h] >= tau, a1 * b_ref[h], 0.0)
        rows = slice(e * PEER_KEYS, (e + 1) * PEER_KEYS)
        hw_ref[rows, :] = act[rows] * w
    o_ref[...] += lax.dot_general(hw_ref[...].astype(bf16), v_ref[...], (((0,), (0,)), ((), ())),
                                  preferred_element_type=f32)


def _peer_mix(x, u_tab, v_tab, s1, s2, a, b, tau):
    t, d = x.shape
    n_exp = u_tab.shape[0]
    tt = _pick(t, (512, 256, 128))
    et = 256
    sel = pl.BlockSpec((PEER_HEADS, PEER_KEYS, tt), lambda i, j: (0, 0, i))
    return pl.pallas_call(
        functools.partial(_peer_mix_kernel, et=et),
        out_shape=jax.ShapeDtypeStruct((t, d), f32),
        grid=(t // tt, n_exp // et),
        in_specs=[pl.BlockSpec((tt, d), lambda i, j: (i, 0)),
                  pl.BlockSpec((et, d), lambda i, j: (j, 0)),
                  pl.BlockSpec((et, d), lambda i, j: (j, 0)),
                  sel, sel, sel, sel,
                  pl.BlockSpec((PEER_HEADS, tt), lambda i, j: (0, i))],
        out_specs=pl.BlockSpec((tt, d), lambda i, j: (i, 0)),
        scratch_shapes=[pltpu.VMEM((et, tt), f32)],
        compiler_params=_params("parallel", "arbitrary"),
    )(x, u_tab, v_tab, s1, s2, a, b, tau)


def _prep_layer_weights(i, w_in, ssd_dt_bias, w_ssd_out, w_pool_out, w_gmlp_out, w_conf_out, w_o,
                        peer_w_q, peer_sub_keys, peer_u, peer_v, ple_w_gate, ple_w_proj, sizes):
    cast = lambda w: w.astype(bf16)
    bounds = [0]
    for s in sizes:
        bounds.append(bounds[-1] + s)
    seg = [cast(w_in[i][:, bounds[k]:bounds[k + 1]]) for k in range(len(sizes))]
    n_heads = sizes[2]
    w_dt = jnp.pad(seg[2], ((0, 0), (0, LANES - n_heads)))
    dt_bias = jnp.zeros((1, LANES), f32).at[0, :n_heads].set(ssd_dt_bias[i])
    return dict(w_z=seg[0], w_xbc=seg[1], w_dt=w_dt, dt_bias=dt_bias, w_pool=seg[3], w_gmlp=seg[4], w_conf=seg[5],
                w_gate=seg[6], w_ssd_out=cast(w_ssd_out[i]), w_pool_out=cast(w_pool_out[i]),
                w_gmlp_out=cast(w_gmlp_out[i]), w_conf_out=cast(w_conf_out[i]), w_o=cast(w_o[i]),
                w_q=cast(peer_w_q[i]), keys=cast(peer_sub_keys[i]), u=cast(peer_u[i]), v=cast(peer_v[i]),
                w_ple_gate=cast(ple_w_gate[i]), w_ple_proj=cast(ple_w_proj[i]))


def _tail_rows(hist, x3, keep):
    seq_len = x3.shape[1]
    if seq_len >= keep:
        return x3[:, seq_len - keep:]
    return jnp.concatenate([hist[:, hist.shape[1] - (keep - seq_len):], x3], axis=1)


def _decoder_layer(x, xb, p_b, conv_hist, h0, pool_hist, conf_hist, first_pos, n_batch, seq_len, wts, lp, alpha):
    d_model = x.shape[1]
    inner = lp['ssd_norm_g'].shape[0]
    xbc_w = lp['ssd_conv_w'].shape[1]
    conf_w = lp['conf_w'].shape[1]

    ident = lambda acc: acc
    z_act = _mm([xb], [(0, wts['w_z'], 0)], inner, lambda acc: _silu(acc))
    xbc_raw = _mm([xb], [(0, wts['w_xbc'], 0)], xbc_w, ident)
    dt = _mm([xb], [(0, wts['w_dt'], 0)], LANES, lambda acc, bias: _softplus(acc + bias),
             extras=[(wts['dt_bias'], 'row', 0)])
    pool_u = _mm([xb], [(0, wts['w_pool'], 0)], wts['w_pool'].shape[1], ident)
    gmlp_h = _mm([xb], [(0, wts['w_gmlp'], 0)], wts['w_gmlp'].shape[1], lambda acc: _gelu(acc))
    glu = _mm([xb], [(0, wts['w_conf'], 0), (0, wts['w_conf'], conf_w)], conf_w,
              lambda ca, cg: ca * _sigmoid(cg))
    gates = _mm([xb], [(0, wts['w_gate'], 0)], N_BRANCH * d_model, lambda acc: _sigmoid(acc))

    xbc_act = _causal_conv(xbc_raw, conv_hist, lp['ssd_conv_w'], lp['ssd_conv_b'], n_batch, seq_len, act='silu')
    y_ssd, h_new = _ssd(xbc_act, dt, z_act, lp['ssd_a_log'], lp['ssd_d'], lp['ssd_norm_g'], h0, n_batch, seq_len)
    conv_new = _tail_rows(conv_hist, xbc_raw.reshape(n_batch, seq_len, xbc_w), SSD_CONV - 1)

    pooled = _multiscale_pool(pool_u, pool_hist, lp['pool_scale'], n_batch, seq_len, first_pos)
    pool_new = _tail_rows(pool_hist, pool_u.reshape(n_batch, seq_len, -1), pool_hist.shape[1])

    a_gmlp, v_gmlp = _gmlp(gmlp_h, lp['gmlp_ln_g'], lp['gmlp_ln_b'], lp['gmlp_ws'], lp['gmlp_bs'], seq_len)

    cconv = _causal_conv(glu, conf_hist, lp['conf_w'], lp['conf_b'], n_batch, seq_len, act=None)
    a_conf, = _layer_norm(cconv, lp['conf_ln_g'], lp['conf_ln_b'], [bf16], act='silu')
    conf_new = _tail_rows(conf_hist, glu.reshape(n_batch, seq_len, conf_w), CONF_CONV - 1)

    merged = _merge(y_ssd, pooled, a_gmlp, a_conf, wts['w_ssd_out'], wts['w_pool_out'], wts['w_gmlp_out'],
                    wts['w_conf_out'], gates, d_model)
    pre1 = _mm([merged], [(0, wts['w_o'], 0)], d_model, lambda acc, res: alpha * res + acc, extras=[(x, 'tile', 0)])
    x1, x1b = _layer_norm(pre1, lp['ln_mix_g'], lp['ln_mix_b'], [f32, bf16])

    q = _mm([x1b], [(0, wts['w_q'], 0)], wts['w_q'].shape[1], ident, out_dtype=bf16)
    s1, s2, ga, gb, tau = _peer_select(q, wts['keys'])
    ffn = _peer_mix(x1b, wts['u'], wts['v'], s1, s2, ga, gb, tau)
    pre2 = _mm([x1b, p_b], [(0, wts['w_ple_gate'], 0), (1, wts['w_ple_proj'], 0)], d_model,
               lambda g, pr, res, f: alpha * res + f + _sigmoid(g) * pr,
               extras=[(x1, 'tile', 0), (ffn, 'tile', 0)])
    x2, x2b = _layer_norm(pre2, lp['ln_ffn_g'], lp['ln_ffn_b'], [f32, bf16])
    return x2, x2b, conv_new, h_new, pool_new, conf_new, v_gmlp


def kernel(x_prompt, x_sample, p_prompt, p_sample, state_ssd_conv, state_ssd_ssm, state_pool, state_conf, w_in, ssd_conv_w, ssd_conv_b, ssd_dt_bias, ssd_a_log, ssd_d, ssd_norm_g, w_ssd_out, pool_scale, w_pool_out, gmlp_ln_g, gmlp_ln_b, gmlp_ws, gmlp_bs, w_gmlp_out, conf_w, conf_b, conf_ln_g, conf_ln_b, w_conf_out, w_o, ln_mix_g, ln_mix_b, peer_w_q, peer_sub_keys, peer_u, peer_v, ple_w_gate, ple_w_proj, ln_ffn_g, ln_ffn_b):
    depth = w_in.shape[0]
    n_b, seq, d_model = x_prompt.shape
    n_db, dseq, _ = x_sample.shape
    past_len = 16384
    alpha = (2.0 * depth) ** 0.25
    n_heads = ssd_a_log.shape[1]
    inner = ssd_norm_g.shape[1]
    xbc_w = ssd_conv_w.shape[2]
    pool_w = pool_scale.shape[1]
    gmlp_w = gmlp_ln_g.shape[1]
    conf_width = conf_w.shape[2]
    sizes = (inner, xbc_w, n_heads, pool_w, 2 * gmlp_w, 2 * conf_width, N_BRANCH * d_model)

    yp = x_prompt.reshape(n_b * seq, d_model)
    ys = x_sample.reshape(n_db * dseq, d_model)
    ypb, ysb = yp.astype(bf16), ys.astype(bf16)
    zeros = lambda h, c: jnp.zeros((n_b, h, c), f32)
    outs_p, outs_s = [], []
    for i in range(depth):
        wts = _prep_layer_weights(i, w_in, ssd_dt_bias, w_ssd_out, w_pool_out, w_gmlp_out, w_conf_out, w_o,
                                  peer_w_q, peer_sub_keys, peer_u, peer_v, ple_w_gate, ple_w_proj, sizes)
        lp = dict(ssd_conv_w=ssd_conv_w[i], ssd_conv_b=ssd_conv_b[i], ssd_a_log=ssd_a_log[i], ssd_d=ssd_d[i],
                  ssd_norm_g=ssd_norm_g[i], pool_scale=pool_scale[i], gmlp_ln_g=gmlp_ln_g[i], gmlp_ln_b=gmlp_ln_b[i],
                  gmlp_ws=gmlp_ws[i], gmlp_bs=gmlp_bs[i], conf_w=conf_w[i], conf_b=conf_b[i],
                  conf_ln_g=conf_ln_g[i], conf_ln_b=conf_ln_b[i], ln_mix_g=ln_mix_g[i], ln_mix_b=ln_mix_b[i],
                  ln_ffn_g=ln_ffn_g[i], ln_ffn_b=ln_ffn_b[i])
        yp, ypb, *st_p = _decoder_layer(
            yp, ypb, p_prompt[i].reshape(n_b * seq, -1).astype(bf16),
            zeros(SSD_CONV - 1, xbc_w), None, zeros(max(POOL_WINDOWS) - 1, pool_w), zeros(CONF_CONV - 1, conf_width),
            0, n_b, seq, wts, lp, alpha)
        ys, ysb, *st_s = _decoder_layer(
            ys, ysb, p_sample[i].reshape(n_db * dseq, -1).astype(bf16),
            state_ssd_conv[i], state_ssd_ssm[i], state_pool[i], state_conf[i],
            past_len, n_db, dseq, wts, lp, alpha)
        outs_p.append(st_p)
        outs_s.append(st_s)
    stack = lambda outs, k: jnp.stack([o[k] for o in outs])
    return (yp.reshape(n_b, seq, d_model), ys.reshape(n_db, dseq, d_model),
            stack(outs_p, 0), stack(outs_p, 1), stack(outs_p, 2), stack(outs_p, 3),
            stack(outs_s, 0), stack(outs_s, 1), stack(outs_s, 2), stack(outs_s, 3),
            stack(outs_s, 4).reshape(depth, n_db, dseq, gmlp_w))
```

```python
import functools
import math

import jax
import jax.numpy as jnp
from jax import lax
from jax.experimental import pallas as pl
from jax.experimental.pallas import tpu as pltpu

f32 = jnp.float32
bf16 = jnp.bfloat16

V7X_VMEM_BYTES = 64 * 1024 * 1024
VMEM_LIMIT = V7X_VMEM_BYTES - 8 * 1024 * 1024
LANES = 128
SUBLANES = 8

NORM_EPS = 1e-5
SSD_HEADDIM = 64
SSD_GROUPS = 4
SSD_STATE = 128
SSD_CONV = 4
POOL_WINDOWS = (2, 4, 8, 16)
GMLP_GROUPS = 8
CONF_CONV = 31
N_BRANCH = 4
PEER_KEYS = 128
PEER_HEADS = 8
PEER_TOPK = 16
PAST_LEN = 16384
ROWS = 128
THRESHOLD_SLACK = 2.0 ** -22
GATE_ROWS = 128


def _params(*sem):
    return pltpu.CompilerParams(dimension_semantics=sem, vmem_limit_bytes=VMEM_LIMIT)


def _pick(n, prefs):
    for p in prefs:
        if n % p == 0:
            return p
    return n


def _sigmoid(x):
    return 1.0 / (1.0 + jnp.exp(-x))


def _silu(x):
    return x * _sigmoid(x)


def _gelu(x):
    return 0.5 * x * (1.0 + lax.erf(x * (1.0 / math.sqrt(2.0))))


def _softplus(x):
    return jnp.maximum(x, 0.0) + jnp.log(1.0 + jnp.exp(-jnp.abs(x)))


def _mm_kernel(*refs, n_a, w_src, n_extra, epi):
    a_refs = refs[:n_a]
    w_refs = refs[n_a:n_a + len(w_src)]
    e_refs = refs[n_a + len(w_src):n_a + len(w_src) + n_extra]
    o_ref = refs[-1]
    a_vals = [r[...] for r in a_refs]
    accs = [jnp.dot(a_vals[s], w[...], preferred_element_type=f32) for s, w in zip(w_src, w_refs)]
    o_ref[...] = epi(*accs, *[e[...] for e in e_refs]).astype(o_ref.dtype)


def _mm(name, layer, a_list, w_list, n_out, epi, extras=(), out_dtype=f32, tm=None, tn=None):
    m = a_list[0].shape[0]
    tm = tm or _pick(m, (1024, 512, 256, 128))
    tn = tn or _pick(n_out, (512, 256, 128))
    in_specs, args = [], []
    for a in a_list:
        in_specs.append(pl.BlockSpec((tm, a.shape[1]), lambda j, i: (i, 0)))
        args.append(a)
    for (_, w, off) in w_list:
        ob = off // tn
        in_specs.append(pl.BlockSpec((None, w.shape[1], tn), lambda j, i, ob=ob: (layer, 0, j + ob)))
        args.append(w)
    for (e, kind, off) in extras:
        ob = off // tn
        if kind == 'tile':
            in_specs.append(pl.BlockSpec((tm, tn), lambda j, i, ob=ob: (i, j + ob)))
        else:
            in_specs.append(pl.BlockSpec((1, tn), lambda j, i, ob=ob: (0, j + ob)))
        args.append(e)
    kern = functools.partial(_mm_kernel, n_a=len(a_list), w_src=tuple(s for s, _, _ in w_list),
                             n_extra=len(extras), epi=epi)
    return pl.pallas_call(
        kern,
        out_shape=jax.ShapeDtypeStruct((m, n_out), out_dtype),
        grid=(n_out // tn, m // tm),
        in_specs=in_specs,
        out_specs=pl.BlockSpec((tm, tn), lambda j, i: (i, j)),
        compiler_params=_params("parallel", "arbitrary"),
        name=name,
    )(*args)


def _ln_kernel(x_ref, g_ref, b_ref, *o_refs, act):
    x = x_ref[...]
    mu = jnp.mean(x, axis=-1, keepdims=True)
    xc = x - mu
    var = jnp.mean(xc * xc, axis=-1, keepdims=True)
    y = xc * lax.rsqrt(var + NORM_EPS) * g_ref[...] + b_ref[...]
    if act == 'silu':
        y = _silu(y)
    for o in o_refs:
        o[...] = y.astype(o.dtype)


def _layer_norm(x, g, b, out_dtypes, act=None):
    m, n = x.shape
    tm = _pick(m, (256, 128))
    outs = pl.pallas_call(
        functools.partial(_ln_kernel, act=act),
        out_shape=[jax.ShapeDtypeStruct((m, n), d) for d in out_dtypes],
        grid=(m // tm,),
        in_specs=[pl.BlockSpec((tm, n), lambda i: (i, 0)),
                  pl.BlockSpec((1, n), lambda i: (0, 0)),
                  pl.BlockSpec((1, n), lambda i: (0, 0))],
        out_specs=[pl.BlockSpec((tm, n), lambda i: (i, 0)) for _ in out_dtypes],
        compiler_params=_params("parallel"),
        name="layer_norm",
    )(x, g.reshape(1, n), b.reshape(1, n))
    return outs


def _hist_pad(h):
    return -(-h // SUBLANES) * SUBLANES


def _fill_window(xc_ref, x_ref, h_ref, s, seq_len, hist):
    hp = _hist_pad(hist)
    r0 = pl.multiple_of(s * seq_len, SUBLANES)
    xc_ref[pl.ds(hp - hist, hist), :] = h_ref[s]
    xc_ref[pl.ds(hp, seq_len), :] = x_ref[pl.ds(r0, seq_len), :]
    return r0


def _conv_kernel(x_ref, h_ref, w_ref, b_ref, o_ref, xc_ref, *, seq_len, width, n_seq, rt, act):
    hist = width - 1
    hp = _hist_pad(hist)
    w = w_ref[...]
    bias = b_ref[...]

    def one_seq(s, carry):
        r0 = _fill_window(xc_ref, x_ref, h_ref, s, seq_len, hist)

        def tile(t, c):
            base = pl.multiple_of(t * rt, SUBLANES)
            win = xc_ref[pl.ds(base, rt + hp), :]
            acc = jnp.broadcast_to(bias, (rt, bias.shape[1]))
            for k in range(width):
                off = hp - hist + k
                acc = acc + win[off:off + rt] * w[k:k + 1]
            if act == 'silu':
                acc = _silu(acc)
            o_ref[pl.ds(pl.multiple_of(r0 + base, SUBLANES), rt), :] = acc
            return c

        lax.fori_loop(0, seq_len // rt, tile, 0)
        return carry

    lax.fori_loop(0, n_seq, one_seq, 0)


def _seq_call(name, kernel_fn, x, hist, layer, small, n_batch, seq_len, out_dtype, ct):
    c = x.shape[1]
    hist_len = hist.shape[2]
    n_seq = 1 if seq_len >= ROWS else _pick(n_batch, (16, 8, 4, 2, 1))
    rows = n_seq * seq_len
    hp = _hist_pad(hist_len)
    in_specs = [pl.BlockSpec((rows, ct), lambda b, j: (b, j)),
                pl.BlockSpec((None, n_seq, hist_len, ct), lambda b, j: (layer, b, 0, j))]
    in_specs += [pl.BlockSpec((p.shape[0], ct), lambda b, j: (0, j)) for p in small]
    return pl.pallas_call(
        functools.partial(kernel_fn, seq_len=seq_len, n_seq=n_seq),
        out_shape=jax.ShapeDtypeStruct(x.shape, out_dtype),
        grid=(n_batch // n_seq, c // ct),
        in_specs=in_specs,
        out_specs=pl.BlockSpec((rows, ct), lambda b, j: (b, j)),
        scratch_shapes=[pltpu.VMEM((hp + seq_len, ct), f32)],
        compiler_params=_params("parallel", "parallel"),
        name=name,
    )(x, hist, *small)


def _causal_conv(x, hist, layer, w, b, n_batch, seq_len, act):
    width = w.shape[0]
    rt = _pick(seq_len, (64, 32, 16, 8))
    kern = functools.partial(_conv_kernel, width=width, rt=rt, act=act)
    return _seq_call("causal_conv", kern, x, hist, layer, [w, b.reshape(1, -1)], n_batch, seq_len, f32, ct=256)


def _pool_kernel(x_ref, h_ref, s_ref, o_ref, xc_ref, *, seq_len, n_seq, rt, hist, first_pos):
    hp = _hist_pad(hist)
    scale = s_ref[...]

    def one_seq(s, carry):
        r0 = _fill_window(xc_ref, x_ref, h_ref, s, seq_len, hist)
        for gi, window in enumerate(POOL_WINDOWS):
            @pl.when(pl.program_id(1) == gi)
            def _(window=window):
                def tile(t, c):
                    base = pl.multiple_of(t * rt, SUBLANES)
                    win = xc_ref[pl.ds(base, rt + hp), :]
                    tok = win[hp:hp + rt]
                    acc = tok
                    for k in range(1, window):
                        acc = acc + win[hp - k:hp - k + rt]
                    pos = first_pos + base + lax.broadcasted_iota(jnp.int32, tok.shape, 0)
                    cnt = jnp.minimum(window, pos + 1).astype(f32)
                    o_ref[pl.ds(pl.multiple_of(r0 + base, SUBLANES), rt), :] = (
                        (acc / cnt - tok) * scale).astype(o_ref.dtype)
                    return c

                lax.fori_loop(0, seq_len // rt, tile, 0)
        return carry

    lax.fori_loop(0, n_seq, one_seq, 0)


def _multiscale_pool(u, hist, layer, scale, n_batch, seq_len, first_pos):
    gw = u.shape[1] // len(POOL_WINDOWS)
    rt = _pick(seq_len, (64, 32, 16, 8))
    kern = functools.partial(_pool_kernel, rt=rt, hist=hist.shape[2], first_pos=first_pos)
    return _seq_call("pool", kern, u, hist, layer, [scale.reshape(1, -1)], n_batch, seq_len, bf16, ct=gw)


def _seg_scan(x, seg_len, reverse):
    n = x.shape[0]
    pos = lax.broadcasted_iota(jnp.int32, x.shape, 0) % seg_len
    s = 1
    while s < seg_len:
        if reverse:
            x = x + jnp.where(pos < seg_len - s, pltpu.roll(x, n - s, axis=0), 0.0)
        else:
            x = x + jnp.where(pos >= s, pltpu.roll(x, s, axis=0), 0.0)
        s *= 2
    return x


def _ssd_kernel(xs_ref, b_ref, c_ref, dt_ref, z_ref, alog_ref, dx_ref, g_ref, *rest, seq_len, n_seq, carry_state):
    if carry_state:
        y_ref, hout_ref, h_scr = rest
    else:
        h0_ref, y_ref, hout_ref = rest
    grp = pl.program_id(1)
    width = xs_ref.shape[1]
    hpg = width // SSD_HEADDIM
    rows = xs_ref.shape[0]
    nt = (((1,), (1,)), ((), ()))
    tn = (((0,), (0,)), ((), ()))

    xs = xs_ref[...]
    bm = b_ref[...].astype(bf16)
    cm = c_ref[...].astype(bf16)

    to_front = (LANES - grp * hpg) % LANES
    dt_all = dt_ref[...]
    dt = pltpu.roll(dt_all, to_front, axis=1)
    da = pltpu.roll(dt_all * (-jnp.exp(alog_ref[...])), to_front, axis=1)
    a_cum = _seg_scan(da, seq_len, reverse=False)
    to_end = _seg_scan(da, seq_len, reverse=True) - da

    lane = lax.broadcasted_iota(jnp.int32, (rows, LANES), 1)
    column = lambda v, r: jnp.broadcast_to(v[:, r:r + 1], (rows, LANES))
    heads_per_tile = LANES // SSD_HEADDIM

    def spread(v):
        tiles = []
        for k in range(width // LANES):
            t = column(v, k * heads_per_tile)
            for p in range(1, heads_per_tile):
                t = jnp.where(lane >= p * SSD_HEADDIM, column(v, k * heads_per_tile + p), t)
            tiles.append(t)
        return jnp.concatenate(tiles, axis=1)

    dtx = spread(dt)
    a_cum_x = spread(a_cum)
    to_end_x = spread(to_end)

    xdt = xs * dtx
    xdt_b = xdt.astype(bf16)

    a_cum_t = a_cum.T
    cb = lax.dot_general(cm, bm, nt, preferred_element_type=f32)
    ri = lax.broadcasted_iota(jnp.int32, (rows, rows), 0)
    ci = lax.broadcasted_iota(jnp.int32, (rows, rows), 1)
    allowed = (ri >= ci) & ((ri // seq_len) == (ci // seq_len))
    col_head = lax.broadcasted_iota(jnp.int32, (rows, width), 1) // SSD_HEADDIM
    y = jnp.zeros((rows, width), f32)
    for r in range(hpg):
        a_col = column(a_cum, r)
        a_row = a_cum_t[r:r + 1, :]
        seg = jnp.exp(jnp.where(allowed, a_col - a_row, -jnp.inf))
        m_r = (cb * seg).astype(bf16)
        y_r = jnp.dot(m_r, xdt_b, preferred_element_type=f32)
        y = y + jnp.where(col_head == r, y_r, 0.0)

    xdt_end = (xdt * jnp.exp(to_end_x)).astype(bf16)
    total_x = a_cum_x + to_end_x
    if carry_state:
        chunk = pl.program_id(2)

        @pl.when(chunk == 0)
        def _():
            h_scr[...] = jnp.zeros_like(h_scr)

        h_t = h_scr[...]
        y_off = jnp.dot(cm, h_t.astype(bf16), preferred_element_type=f32)
        st_t = lax.dot_general(bm, xdt_end, tn, preferred_element_type=f32)
        h_new = h_t * jnp.exp(total_x[0:1, :]) + st_t
        h_scr[...] = h_new

        @pl.when(chunk == pl.num_programs(2) - 1)
        def _():
            hout_ref[...] = h_new.T.reshape(hout_ref.shape)
    else:
        h_prev = h0_ref[...].reshape(n_seq * width, SSD_STATE)
        y_all = lax.dot_general(cm, h_prev.astype(bf16), nt, preferred_element_type=f32)
        decay_t = jnp.exp(total_x).T
        row_seq = lax.broadcasted_iota(jnp.int32, (rows, 1), 0) // seq_len
        y_off = jnp.zeros((rows, width), f32)
        for s in range(n_seq):
            in_seq = row_seq == s
            y_off = y_off + jnp.where(in_seq, y_all[:, s * width:(s + 1) * width], 0.0)
            st = lax.dot_general(jnp.where(in_seq, xdt_end, jnp.zeros_like(xdt_end)), bm, tn,
                                 preferred_element_type=f32)
            decay = jnp.broadcast_to(decay_t[:, s * seq_len:s * seq_len + 1], (width, SSD_STATE))
            h_new = h_prev[s * width:(s + 1) * width] * decay + st
            hout_ref[s] = h_new.reshape(hout_ref.shape[1:])
    y = y + y_off * jnp.exp(a_cum_x)

    y = (y + xs * dx_ref[...]) * z_ref[...]
    y = y * lax.rsqrt(jnp.mean(y * y, axis=-1, keepdims=True) + NORM_EPS) * g_ref[...]
    y_ref[...] = y.astype(y_ref.dtype)


def _ssd(xbc, dt, z_act, a_log, d_skip, norm_g, h0, layer, n_batch, seq_len):
    t = xbc.shape[0]
    n_heads = a_log.shape[0]
    inner = n_heads * SSD_HEADDIM
    width = inner // SSD_GROUPS
    hpg = n_heads // SSD_GROUPS
    carry = seq_len >= ROWS
    n_seq = 1 if carry else ROWS // seq_len
    n_chunks = seq_len // ROWS if carry else 1
    n_bblk = n_batch if carry else n_batch // n_seq
    alog_pad = jnp.zeros((1, LANES), f32).at[0, :n_heads].set(a_log)
    b_blk0 = inner // SSD_STATE
    c_blk0 = b_blk0 + SSD_GROUPS
    row = lambda b, g, c: b * n_chunks + c
    state_blk = (n_seq, hpg, SSD_HEADDIM, SSD_STATE)
    in_specs = [
        pl.BlockSpec((ROWS, width), lambda b, g, c: (row(b, g, c), g)),
        pl.BlockSpec((ROWS, SSD_STATE), lambda b, g, c: (row(b, g, c), b_blk0 + g)),
        pl.BlockSpec((ROWS, SSD_STATE), lambda b, g, c: (row(b, g, c), c_blk0 + g)),
        pl.BlockSpec((ROWS, LANES), lambda b, g, c: (row(b, g, c), 0)),
        pl.BlockSpec((ROWS, width), lambda b, g, c: (row(b, g, c), g)),
        pl.BlockSpec((1, LANES), lambda b, g, c: (0, 0)),
        pl.BlockSpec((1, width), lambda b, g, c: (0, g)),
        pl.BlockSpec((1, width), lambda b, g, c: (0, g)),
    ]
    args = [xbc, xbc, xbc, dt, z_act, alog_pad, jnp.repeat(d_skip, SSD_HEADDIM).reshape(1, inner),
            norm_g.reshape(1, inner)]
    scratch = []
    if carry:
        assert h0 is None
        scratch = [pltpu.VMEM((SSD_STATE, width), f32)]
    else:
        in_specs.append(pl.BlockSpec((None,) + state_blk, lambda b, g, c: (layer, b, g, 0, 0)))
        args.append(h0)
    kern = functools.partial(_ssd_kernel, seq_len=ROWS if carry else seq_len, n_seq=n_seq, carry_state=carry)
    y, h_final = pl.pallas_call(
        kern,
        out_shape=[jax.ShapeDtypeStruct((t, inner), bf16),
                   jax.ShapeDtypeStruct((n_batch, n_heads, SSD_HEADDIM, SSD_STATE), f32)],
        grid=(n_bblk, SSD_GROUPS, n_chunks),
        in_specs=in_specs,
        out_specs=[pl.BlockSpec((ROWS, width), lambda b, g, c: (row(b, g, c), g)),
                   pl.BlockSpec(state_blk, lambda b, g, c: (b, g, 0, 0))],
        scratch_shapes=scratch,
        compiler_params=_params("parallel", "parallel", "arbitrary"),
        name="ssd",
    )(*args)
    return y, h_final


def _gmlp_kernel(h_ref, g_ref, b_ref, ws_ref, bias_ref, o_ref, v_ref):
    half = h_ref.shape[1] // 2
    u = h_ref[:, :half]
    v = h_ref[:, half:]
    mu = jnp.mean(v, axis=-1, keepdims=True)
    vc = v - mu
    var = jnp.mean(vc * vc, axis=-1, keepdims=True)
    v = vc * lax.rsqrt(var + NORM_EPS) * g_ref[...] + b_ref[...]
    v_ref[...] = v
    vb = v.astype(bf16)
    gd = half // GMLP_GROUPS
    for g in range(GMLP_GROUPS):
        cs = slice(g * gd, (g + 1) * gd)
        mixed = jnp.dot(ws_ref[g], vb[:, cs], preferred_element_type=f32) + bias_ref[:, cs]
        o_ref[:, cs] = (u[:, cs] * mixed).astype(o_ref.dtype)


def _gmlp(h, ln_g, ln_b, ws, bs, seq_len):
    t, two_w = h.shape
    half = two_w // 2
    q = ROWS if seq_len % ROWS == 0 else seq_len
    reps = ROWS // q
    tri = jnp.tril(jnp.ones((q, q), f32))
    ws_q = ws[:, :q, :q] * tri
    ws_blk = jnp.einsum('ab,gij->gaibj', jnp.eye(reps, dtype=f32), ws_q).reshape(GMLP_GROUPS, ROWS, ROWS).astype(bf16)
    bias = jnp.repeat(jnp.tile(bs[:, :q], (1, reps)).T, half // GMLP_GROUPS, axis=1)
    out, v = pl.pallas_call(
        _gmlp_kernel,
        out_shape=[jax.ShapeDtypeStruct((t, half), bf16), jax.ShapeDtypeStruct((t, half), f32)],
        grid=(t // ROWS,),
        in_specs=[pl.BlockSpec((ROWS, two_w), lambda i: (i, 0)),
                  pl.BlockSpec((1, half), lambda i: (0, 0)),
                  pl.BlockSpec((1, half), lambda i: (0, 0)),
                  pl.BlockSpec((GMLP_GROUPS, ROWS, ROWS), lambda i: (0, 0, 0)),
                  pl.BlockSpec((ROWS, half), lambda i: (0, 0))],
        out_specs=[pl.BlockSpec((ROWS, half), lambda i: (i, 0)), pl.BlockSpec((ROWS, half), lambda i: (i, 0))],
        compiler_params=_params("parallel"),
        name="gmlp",
    )(h, ln_g.reshape(1, half), ln_b.reshape(1, half), ws_blk, bias)
    return out, v


def _merge_kernel(a0, a1, a2, a3, w0, w1, w2, w3, g0, g1, g2, g3, o_ref):
    acc = g0[...].astype(f32) * jnp.dot(a0[...], w0[...], preferred_element_type=f32)
    acc += g1[...].astype(f32) * jnp.dot(a1[...], w1[...], preferred_element_type=f32)
    acc += g2[...].astype(f32) * jnp.dot(a2[...], w2[...], preferred_element_type=f32)
    acc += g3[...].astype(f32) * jnp.dot(a3[...], w3[...], preferred_element_type=f32)
    o_ref[...] = acc.astype(o_ref.dtype)


def _merge(layer, a_ssd, a_pool, a_gmlp, a_conf, w_ssd, w_pool, w_gmlp, w_conf, gates, d_model):
    t = a_ssd.shape[0]
    tm = _pick(t, (512, 256, 128))
    tn = 512
    grp_in = w_pool.shape[2]
    per_grp = w_pool.shape[3] // tn
    nb = d_model // tn
    full = lambda a: pl.BlockSpec((tm, a.shape[1]), lambda j, i: (i, 0))
    wcol = lambda w: pl.BlockSpec((None, w.shape[1], tn), lambda j, i: (layer, 0, j))
    gate = lambda k: pl.BlockSpec((tm, tn), lambda j, i, k=k: (i, k * nb + j))
    return pl.pallas_call(
        _merge_kernel,
        out_shape=jax.ShapeDtypeStruct((t, d_model), bf16),
        grid=(nb, t // tm),
        in_specs=[full(a_ssd),
                  pl.BlockSpec((tm, grp_in), lambda j, i: (i, j // per_grp)),
                  full(a_gmlp), full(a_conf),
                  wcol(w_ssd),
                  pl.BlockSpec((None, None, grp_in, tn), lambda j, i: (layer, j // per_grp, 0, j % per_grp)),
                  wcol(w_gmlp), wcol(w_conf),
                  gate(0), gate(1), gate(2), gate(3)],
        out_specs=pl.BlockSpec((tm, tn), lambda j, i: (i, j)),
        compiler_params=_params("parallel", "arbitrary"),
        name="merge",
    )(a_ssd, a_pool, a_gmlp, a_conf, w_ssd, w_pool, w_gmlp, w_conf, gates, gates, gates, gates)


def _top_values(work, out_ref, count):
    for r in range(count):
        m = jnp.max(work, axis=0, keepdims=True)
        out_ref[pl.ds(r, 1), :] = m
        work = jnp.where(work == m, -jnp.inf, work)


def _candidate_rows(k):
    return [(i, k // (i + 1)) for i in range(k)]


def _peer_select_kernel(q_ref, k_ref, thr_ref, s2_ref, a_ref, b_ref, top1, top2, cand):
    k = PEER_TOPK
    cand[...] = jnp.full(cand.shape, -jnp.inf, f32)
    for h in range(PEER_HEADS):
        halves = []
        for c in range(2):
            qb = q_ref[:, (2 * h + c) * LANES:(2 * h + c + 1) * LANES]
            halves.append(lax.dot_general(k_ref[h, c], qb, (((1,), (1,)), ((), ())), preferred_element_type=f32))
        s1, s2 = halves
        _top_values(s1, top1, k)
        _top_values(s2, top2, k)
        off = 0
        for i, n_j in _candidate_rows(k):
            cand[pl.ds(off, n_j), :] = top1[pl.ds(i, 1), :] + top2[pl.ds(0, n_j), :]
            off += n_j
        cv = cand[...]
        work = cv
        for r in range(k):
            tau = jnp.max(work, axis=0, keepdims=True)
            work = jnp.where(work == tau, -jnp.inf, work)
        m1 = top1[pl.ds(0, 1), :]
        m2 = top2[pl.ds(0, 1), :]
        z = jnp.sum(jnp.where(cv >= tau, jnp.exp(cv - (m1 + m2)), 0.0), axis=0, keepdims=True)
        thr_ref[h] = (tau - s1) - THRESHOLD_SLACK * (jnp.abs(tau) + jnp.abs(s1))
        s2_ref[h] = s2
        a_ref[h] = jnp.exp(s1 - m1) / z
        b_ref[h] = jnp.exp(s2 - m2)


def _peer_select(layer, q, keys):
    t = q.shape[0]
    tt = _pick(t, (256, 128))
    n_cand = sum(n for _, n in _candidate_rows(PEER_TOPK))
    big = jax.ShapeDtypeStruct((PEER_HEADS, PEER_KEYS, t), f32)
    blk = pl.BlockSpec((PEER_HEADS, PEER_KEYS, tt), lambda i: (0, 0, i))
    return pl.pallas_call(
        _peer_select_kernel,
        out_shape=[big, big, big, big],
        grid=(t // tt,),
        in_specs=[pl.BlockSpec((tt, q.shape[1]), lambda i: (i, 0)),
                  pl.BlockSpec((None,) + keys.shape[1:], lambda i: (layer, 0, 0, 0, 0))],
        out_specs=[blk, blk, blk, blk],
        scratch_shapes=[pltpu.VMEM((PEER_TOPK, tt), f32), pltpu.VMEM((PEER_TOPK, tt), f32),
                        pltpu.VMEM((_hist_pad(n_cand), tt), f32)],
        compiler_params=_params("parallel"),
        name="peer_select",
    )(q, keys)


def _peer_mix_kernel(x_ref, u_ref, v_ref, thr_ref, s2_ref, a_ref, b_ref, o_ref, ht_ref, hw_ref, *, et, sub, pieces):
    j = pl.program_id(1)

    @pl.when(j == 0)
    def _():
        o_ref[...] = jnp.zeros_like(o_ref)

    n_e1 = sub // PEER_KEYS
    n_chain = et // sub
    row0 = (j * (et // PEER_KEYS)) % SUBLANES
    tt, d = x_ref.shape
    nt = (((1,), (1,)), ((), ()))
    tn = (((0,), (0,)), ((), ()))
    kc = d // pieces
    nc = d // pieces

    def pre_act(c, p):
        rows, ks = pl.ds(c * sub, sub), pl.ds(p * kc, kc)
        part = lax.dot_general(x_ref[:, ks], u_ref[rows, ks], nt, preferred_element_type=f32).T
        if p == 0:
            ht_ref[rows, :] = part
        else:
            ht_ref[rows, :] += part

    def gate(c, p):
        blocks = [(e, lc) for e in range(n_e1) for lc in range(tt // LANES)]
        per = -(-len(blocks) // pieces)
        for e, lc in blocks[p * per:(p + 1) * per]:
            r = row0 + c * n_e1 + e
            tk = pl.ds(lc * LANES, LANES)
            for half in range(PEER_KEYS // GATE_ROWS):
                k2 = pl.ds(half * GATE_ROWS, GATE_ROWS)
                er = pl.ds(c * sub + e * PEER_KEYS + half * GATE_ROWS, GATE_ROWS)
                w = jnp.zeros((GATE_ROWS, LANES), f32)
                for h in range(PEER_HEADS):
                    hit = s2_ref[h, k2, tk] >= thr_ref[h, pl.ds(r, 1), :][:, lc * LANES:(lc + 1) * LANES]
                    a1 = a_ref[h, pl.ds(r, 1), :][:, lc * LANES:(lc + 1) * LANES]
                    w = w + jnp.where(hit, a1 * b_ref[h, k2, tk], 0.0)
                hw_ref[er, tk] = (_gelu(ht_ref[er, tk]) * w).astype(bf16)

    def mix(c, p):
        rows, ns = pl.ds(c * sub, sub), pl.ds(p * nc, nc)
        o_ref[:, ns] += lax.dot_general(hw_ref[rows, :], v_ref[rows, ns], tn, preferred_element_type=f32)

    for p in range(pieces):
        pre_act(0, p)
    for c in range(n_chain):
        for p in range(pieces):
            if c + 1 < n_chain:
                pre_act(c + 1, p)
            if c > 0:
                mix(c - 1, p)
            gate(c, p)
    for p in range(pieces):
        mix(n_chain - 1, p)


def _peer_mix(layer, x, u_tab, v_tab, thr, s2, a, b):
    t, d = x.shape
    n_exp = u_tab.shape[1]
    tt = _pick(t, (512, 256, 128))
    et, sub = 512, 256
    per_blk = SUBLANES * PEER_KEYS // et
    row_blk = lambda arr: arr.reshape(PEER_HEADS, PEER_KEYS // SUBLANES, SUBLANES, t)
    one = pl.BlockSpec((PEER_HEADS, None, SUBLANES, tt), lambda i, j: (0, j // per_blk, 0, i))
    full = pl.BlockSpec((PEER_HEADS, PEER_KEYS, tt), lambda i, j: (0, 0, i))
    tab = pl.BlockSpec((None, et, d), lambda i, j: (layer, j, 0))
    return pl.pallas_call(
        functools.partial(_peer_mix_kernel, et=et, sub=sub, pieces=4),
        out_shape=jax.ShapeDtypeStruct((t, d), f32),
        grid=(t // tt, n_exp // et),
        in_specs=[pl.BlockSpec((tt, d), lambda i, j: (i, 0)), tab, tab, one, full, one, full],
        out_specs=pl.BlockSpec((tt, d), lambda i, j: (i, 0)),
        scratch_shapes=[pltpu.VMEM((et, tt), f32), pltpu.VMEM((et, tt), bf16)],
        compiler_params=_params("parallel", "arbitrary"),
        name="peer_mix",
    )(x, u_tab, v_tab, row_blk(thr), s2, row_blk(a), b)


def _prep_weights(w_in, ssd_dt_bias, w_ssd_out, w_pool_out, w_gmlp_out, w_conf_out, w_o,
                  peer_w_q, peer_sub_keys, peer_u, peer_v, ple_w_gate, ple_w_proj, sizes):
    cast = lambda w: w.astype(bf16)
    inner, xbc_w, n_heads = sizes[:3]
    dt_off = inner + xbc_w
    tail0 = dt_off + n_heads
    assert dt_off % LANES == 0 and n_heads <= LANES and all(s % LANES == 0 for s in sizes[3:])
    depth = w_in.shape[0]
    dt_bias = jnp.zeros((depth, 1, LANES), f32).at[:, 0, :n_heads].set(ssd_dt_bias)
    dt_mask = (jnp.arange(LANES) < n_heads).astype(f32).reshape(1, LANES)
    return dict(head=cast(w_in[:, :, :dt_off + LANES]), tail=cast(w_in[:, :, tail0:]), dt_off=dt_off,
                dt_bias=dt_bias, dt_mask=dt_mask,
                w_ssd_out=cast(w_ssd_out), w_pool_out=cast(w_pool_out), w_gmlp_out=cast(w_gmlp_out),
                w_conf_out=cast(w_conf_out), w_o=cast(w_o), w_q=cast(peer_w_q), keys=cast(peer_sub_keys),
                u=cast(peer_u), v=cast(peer_v), w_ple_gate=cast(ple_w_gate), w_ple_proj=cast(ple_w_proj))


def _tail_rows(hist, x3, keep):
    seq_len = x3.shape[1]
    if seq_len >= keep:
        return x3[:, seq_len - keep:]
    return jnp.concatenate([hist[:, hist.shape[1] - (keep - seq_len):], x3], axis=1)


def _decoder_layer(layer, x, xb, p_b, conv_hist, h0, pool_hist, conf_hist, hist_layer, first_pos, n_batch, seq_len,
                   wts, lp, alpha):
    d_model = x.shape[1]
    inner = lp['ssd_norm_g'].shape[0]
    xbc_w = lp['ssd_conv_w'].shape[1]
    pool_w = lp['pool_scale'].shape[0]
    gmlp_w = lp['gmlp_ln_g'].shape[0]
    conf_w = lp['conf_w'].shape[1]
    head, tail = wts['head'], wts['tail']
    gmlp_off = pool_w
    conf_off = gmlp_off + 2 * gmlp_w
    gate_off = conf_off + 2 * conf_w

    ident = lambda acc: acc
    z_act = _mm("proj_z", layer, [xb], [(0, head, 0)], inner, lambda acc: _silu(acc))
    xbc_raw = _mm("proj_xbc", layer, [xb], [(0, head, inner)], xbc_w, ident)
    dt = _mm("proj_dt", layer, [xb], [(0, head, wts['dt_off'])], LANES,
             lambda acc, bias, mask: jnp.where(mask > 0.0, _softplus(acc + bias), 0.0),
             extras=[(wts['dt_bias'][layer], 'row', 0), (wts['dt_mask'], 'row', 0)])
    pool_u = _mm("proj_pool", layer, [xb], [(0, tail, 0)], pool_w, ident)
    gmlp_h = _mm("proj_gmlp", layer, [xb], [(0, tail, gmlp_off)], 2 * gmlp_w, lambda acc: _gelu(acc))
    glu = _mm("proj_glu", layer, [xb], [(0, tail, conf_off), (0, tail, conf_off + conf_w)], conf_w,
              lambda ca, cg: ca * _sigmoid(cg))
    gates = _mm("proj_gates", layer, [xb], [(0, tail, gate_off)], N_BRANCH * d_model, lambda acc: _sigmoid(acc),
                out_dtype=bf16)

    xbc_act = _causal_conv(xbc_raw, conv_hist, hist_layer, lp['ssd_conv_w'], lp['ssd_conv_b'], n_batch, seq_len,
                           act='silu')
    y_ssd, h_new = _ssd(xbc_act, dt, z_act, lp['ssd_a_log'], lp['ssd_d'], lp['ssd_norm_g'], h0, hist_layer,
                        n_batch, seq_len)
    conv_new = _tail_rows(conv_hist[hist_layer], xbc_raw.reshape(n_batch, seq_len, xbc_w), SSD_CONV - 1)

    pooled = _multiscale_pool(pool_u, pool_hist, hist_layer, lp['pool_scale'], n_batch, seq_len, first_pos)
    pool_new = _tail_rows(pool_hist[hist_layer], pool_u.reshape(n_batch, seq_len, pool_w), pool_hist.shape[2])

    a_gmlp, v_gmlp = _gmlp(gmlp_h, lp['gmlp_ln_g'], lp['gmlp_ln_b'], lp['gmlp_ws'], lp['gmlp_bs'], seq_len)

    cconv = _causal_conv(glu, conf_hist, hist_layer, lp['conf_w'], lp['conf_b'], n_batch, seq_len, act=None)
    a_conf, = _layer_norm(cconv, lp['conf_ln_g'], lp['conf_ln_b'], [bf16], act='silu')
    conf_new = _tail_rows(conf_hist[hist_layer], glu.reshape(n_batch, seq_len, conf_w), CONF_CONV - 1)

    merged = _merge(layer, y_ssd, pooled, a_gmlp, a_conf, wts['w_ssd_out'], wts['w_pool_out'], wts['w_gmlp_out'],
                    wts['w_conf_out'], gates, d_model)
    pre1 = _mm("out_proj", layer, [merged], [(0, wts['w_o'], 0)], d_model, lambda acc, res: alpha * res + acc,
               extras=[(x, 'tile', 0)])
    x1, x1b = _layer_norm(pre1, lp['ln_mix_g'], lp['ln_mix_b'], [f32, bf16])

    q = _mm("peer_q", layer, [x1b], [(0, wts['w_q'], 0)], wts['w_q'].shape[2], ident, out_dtype=bf16)
    thr, s2, ga, gb = _peer_select(layer, q, wts['keys'])
    ffn = _peer_mix(layer, x1b, wts['u'], wts['v'], thr, s2, ga, gb)
    pre2 = _mm("ple_ffn", layer, [x1b, p_b], [(0, wts['w_ple_gate'], 0), (1, wts['w_ple_proj'], 0)], d_model,
               lambda g, pr, res, f: alpha * res + f + _sigmoid(g) * pr,
               extras=[(x1, 'tile', 0), (ffn, 'tile', 0)])
    x2, x2b = _layer_norm(pre2, lp['ln_ffn_g'], lp['ln_ffn_b'], [f32, bf16])
    return x2, x2b, conv_new, h_new, pool_new, conf_new, v_gmlp


def kernel(x_prompt, x_sample, p_prompt, p_sample, state_ssd_conv, state_ssd_ssm, state_pool, state_conf, w_in, ssd_conv_w, ssd_conv_b, ssd_dt_bias, ssd_a_log, ssd_d, ssd_norm_g, w_ssd_out, pool_scale, w_pool_out, gmlp_ln_g, gmlp_ln_b, gmlp_ws, gmlp_bs, w_gmlp_out, conf_w, conf_b, conf_ln_g, conf_ln_b, w_conf_out, w_o, ln_mix_g, ln_mix_b, peer_w_q, peer_sub_keys, peer_u, peer_v, ple_w_gate, ple_w_proj, ln_ffn_g, ln_ffn_b):
    depth = w_in.shape[0]
    n_b, seq, d_model = x_prompt.shape
    n_db, dseq, _ = x_sample.shape
    alpha = (2.0 * depth) ** 0.25
    n_heads = ssd_a_log.shape[1]
    inner = ssd_norm_g.shape[1]
    xbc_w = ssd_conv_w.shape[2]
    pool_w = pool_scale.shape[1]
    gmlp_w = gmlp_ln_g.shape[1]
    conf_width = conf_w.shape[2]
    sizes = (inner, xbc_w, n_heads, pool_w, 2 * gmlp_w, 2 * conf_width, N_BRANCH * d_model)
    wts = _prep_weights(w_in, ssd_dt_bias, w_ssd_out, w_pool_out, w_gmlp_out, w_conf_out, w_o,
                        peer_w_q, peer_sub_keys, peer_u, peer_v, ple_w_gate, ple_w_proj, sizes)

    yp = x_prompt.reshape(n_b * seq, d_model)
    ys = x_sample.reshape(n_db * dseq, d_model)
    ypb, ysb = yp.astype(bf16), ys.astype(bf16)
    zeros = lambda h, c: jnp.zeros((1, n_b, h, c), f32)
    outs_p, outs_s = [], []
    for i in range(depth):
        lp = dict(ssd_conv_w=ssd_conv_w[i], ssd_conv_b=ssd_conv_b[i], ssd_a_log=ssd_a_log[i], ssd_d=ssd_d[i],
                  ssd_norm_g=ssd_norm_g[i], pool_scale=pool_scale[i], gmlp_ln_g=gmlp_ln_g[i], gmlp_ln_b=gmlp_ln_b[i],
                  gmlp_ws=gmlp_ws[i], gmlp_bs=gmlp_bs[i], conf_w=conf_w[i], conf_b=conf_b[i],
                  conf_ln_g=conf_ln_g[i], conf_ln_b=conf_ln_b[i], ln_mix_g=ln_mix_g[i], ln_mix_b=ln_mix_b[i],
                  ln_ffn_g=ln_ffn_g[i], ln_ffn_b=ln_ffn_b[i])
        yp, ypb, *st_p = _decoder_layer(
            i, yp, ypb, p_prompt[i].reshape(n_b * seq, -1).astype(bf16),
            zeros(SSD_CONV - 1, xbc_w), None, zeros(max(POOL_WINDOWS) - 1, pool_w), zeros(CONF_CONV - 1, conf_width),
            0, 0, n_b, seq, wts, lp, alpha)
        ys, ysb, *st_s = _decoder_layer(
            i, ys, ysb, p_sample[i].reshape(n_db * dseq, -1).astype(bf16),
            state_ssd_conv, state_ssd_ssm, state_pool, state_conf,
            i, PAST_LEN, n_db, dseq, wts, lp, alpha)
        outs_p.append(st_p)
        outs_s.append(st_s)
    stack = lambda outs, k: jnp.stack([o[k] for o in outs])
    return (yp.reshape(n_b, seq, d_model), ys.reshape(n_db, dseq, d_model),
            stack(outs_p, 0), stack(outs_p, 1), stack(outs_p, 2), stack(outs_p, 3),
            stack(outs_s, 0), stack(outs_s, 1), stack(outs_s, 2), stack(outs_s, 3),
            stack(outs_s, 4).reshape(depth, n_db, dseq, gmlp_w))
```

```python
import functools
import math

import jax
import jax.numpy as jnp
from jax import lax
from jax.experimental import pallas as pl
from jax.experimental.pallas import tpu as pltpu

f32 = jnp.float32
bf16 = jnp.bfloat16

V7X_VMEM_BYTES = 64 * 1024 * 1024
VMEM_LIMIT = V7X_VMEM_BYTES - 8 * 1024 * 1024
LANES = 128
SUBLANES = 8

NORM_EPS = 1e-5
SSD_HEADDIM = 64
SSD_GROUPS = 4
SSD_STATE = 128
SSD_CONV = 4
POOL_WINDOWS = (2, 4, 8, 16)
GMLP_GROUPS = 8
CONF_CONV = 31
N_BRANCH = 4
PEER_KEYS = 128
PEER_HEADS = 8
PEER_TOPK = 16
PAST_LEN = 16384
ROWS = 128
THRESHOLD_SLACK = 2.0 ** -22


def _params(*sem):
    return pltpu.CompilerParams(dimension_semantics=sem, vmem_limit_bytes=VMEM_LIMIT)


def _pick(n, prefs):
    for p in prefs:
        if n % p == 0:
            return p
    return n


def _sigmoid(x):
    return 1.0 / (1.0 + jnp.exp(-x))


def _silu(x):
    return x * _sigmoid(x)


def _gelu(x):
    return 0.5 * x * (1.0 + lax.erf(x * (1.0 / math.sqrt(2.0))))


def _softplus(x):
    return jnp.maximum(x, 0.0) + jnp.log(1.0 + jnp.exp(-jnp.abs(x)))


def _mm_kernel(*refs, n_a, w_src, n_extra, epi):
    a_refs = refs[:n_a]
    w_refs = refs[n_a:n_a + len(w_src)]
    e_refs = refs[n_a + len(w_src):n_a + len(w_src) + n_extra]
    o_ref = refs[-1]
    a_vals = [r[...] for r in a_refs]
    accs = [jnp.dot(a_vals[s], w[...], preferred_element_type=f32) for s, w in zip(w_src, w_refs)]
    o_ref[...] = epi(*accs, *[e[...] for e in e_refs]).astype(o_ref.dtype)


def _mm(name, layer, a_list, w_list, n_out, epi, extras=(), out_dtype=f32, tm=None, tn=None):
    m = a_list[0].shape[0]
    tm = tm or _pick(m, (1024, 512, 256, 128))
    tn = tn or _pick(n_out, (512, 256, 128))
    in_specs, args = [], []
    for a in a_list:
        in_specs.append(pl.BlockSpec((tm, a.shape[1]), lambda j, i: (i, 0)))
        args.append(a)
    for (_, w, off) in w_list:
        ob = off // tn
        in_specs.append(pl.BlockSpec((None, w.shape[1], tn), lambda j, i, ob=ob: (layer, 0, j + ob)))
        args.append(w)
    for (e, kind, off) in extras:
        ob = off // tn
        if kind == 'tile':
            in_specs.append(pl.BlockSpec((tm, tn), lambda j, i, ob=ob: (i, j + ob)))
        else:
            in_specs.append(pl.BlockSpec((1, tn), lambda j, i, ob=ob: (0, j + ob)))
        args.append(e)
    kern = functools.partial(_mm_kernel, n_a=len(a_list), w_src=tuple(s for s, _, _ in w_list),
                             n_extra=len(extras), epi=epi)
    return pl.pallas_call(
        kern,
        out_shape=jax.ShapeDtypeStruct((m, n_out), out_dtype),
        grid=(n_out // tn, m // tm),
        in_specs=in_specs,
        out_specs=pl.BlockSpec((tm, tn), lambda j, i: (i, j)),
        compiler_params=_params("parallel", "arbitrary"),
        name=name,
    )(*args)


def _ln_kernel(x_ref, g_ref, b_ref, *o_refs, act):
    x = x_ref[...]
    mu = jnp.mean(x, axis=-1, keepdims=True)
    xc = x - mu
    var = jnp.mean(xc * xc, axis=-1, keepdims=True)
    y = xc * lax.rsqrt(var + NORM_EPS) * g_ref[...] + b_ref[...]
    if act == 'silu':
        y = _silu(y)
    for o in o_refs:
        o[...] = y.astype(o.dtype)


def _layer_norm(x, g, b, out_dtypes, act=None):
    m, n = x.shape
    tm = _pick(m, (256, 128))
    outs = pl.pallas_call(
        functools.partial(_ln_kernel, act=act),
        out_shape=[jax.ShapeDtypeStruct((m, n), d) for d in out_dtypes],
        grid=(m // tm,),
        in_specs=[pl.BlockSpec((tm, n), lambda i: (i, 0)),
                  pl.BlockSpec((1, n), lambda i: (0, 0)),
                  pl.BlockSpec((1, n), lambda i: (0, 0))],
        out_specs=[pl.BlockSpec((tm, n), lambda i: (i, 0)) for _ in out_dtypes],
        compiler_params=_params("parallel"),
        name="layer_norm",
    )(x, g.reshape(1, n), b.reshape(1, n))
    return outs


def _hist_pad(h):
    return -(-h // SUBLANES) * SUBLANES


def _fill_window(xc_ref, x_ref, h_ref, s, seq_len, hist):
    hp = _hist_pad(hist)
    r0 = pl.multiple_of(s * seq_len, SUBLANES)
    xc_ref[pl.ds(hp - hist, hist), :] = h_ref[s]
    xc_ref[pl.ds(hp, seq_len), :] = x_ref[pl.ds(r0, seq_len), :]
    return r0


def _conv_kernel(x_ref, h_ref, w_ref, b_ref, o_ref, xc_ref, *, seq_len, width, n_seq, rt, act):
    hist = width - 1
    hp = _hist_pad(hist)
    w = w_ref[...]
    bias = b_ref[...]

    def one_seq(s, carry):
        r0 = _fill_window(xc_ref, x_ref, h_ref, s, seq_len, hist)

        def tile(t, c):
            base = pl.multiple_of(t * rt, SUBLANES)
            win = xc_ref[pl.ds(base, rt + hp), :]
            acc = jnp.broadcast_to(bias, (rt, bias.shape[1]))
            rotated = {0: win}
            for k in range(width):
                base8, s = divmod(hp - hist + k, SUBLANES)
                if s not in rotated:
                    rotated[s] = pltpu.roll(win, win.shape[0] - s, axis=0)
                acc = acc + rotated[s][base8 * SUBLANES:base8 * SUBLANES + rt] * w[k:k + 1]
            if act == 'silu':
                acc = _silu(acc)
            o_ref[pl.ds(pl.multiple_of(r0 + base, SUBLANES), rt), :] = acc
            return c

        lax.fori_loop(0, seq_len // rt, tile, 0)
        return carry

    lax.fori_loop(0, n_seq, one_seq, 0)


def _seq_call(name, kernel_fn, x, hist, layer, small, n_batch, seq_len, out_dtype, ct):
    c = x.shape[1]
    hist_len = hist.shape[2]
    n_seq = 1 if seq_len >= ROWS else _pick(n_batch, (16, 8, 4, 2, 1))
    rows = n_seq * seq_len
    hp = _hist_pad(hist_len)
    in_specs = [pl.BlockSpec((rows, ct), lambda b, j: (b, j)),
                pl.BlockSpec((None, n_seq, hist_len, ct), lambda b, j: (layer, b, 0, j))]
    in_specs += [pl.BlockSpec((p.shape[0], ct), lambda b, j: (0, j)) for p in small]
    return pl.pallas_call(
        functools.partial(kernel_fn, seq_len=seq_len, n_seq=n_seq),
        out_shape=jax.ShapeDtypeStruct(x.shape, out_dtype),
        grid=(n_batch // n_seq, c // ct),
        in_specs=in_specs,
        out_specs=pl.BlockSpec((rows, ct), lambda b, j: (b, j)),
        scratch_shapes=[pltpu.VMEM((hp + seq_len, ct), f32)],
        compiler_params=_params("parallel", "parallel"),
        name=name,
    )(x, hist, *small)


def _causal_conv(x, hist, layer, w, b, n_batch, seq_len, act):
    width = w.shape[0]
    rt = _pick(seq_len, (64, 32, 16, 8))
    kern = functools.partial(_conv_kernel, width=width, rt=rt, act=act)
    ct = 256 if seq_len >= ROWS else _pick(x.shape[1], (1024, 512, 256))
    return _seq_call("causal_conv", kern, x, hist, layer, [w, b.reshape(1, -1)], n_batch, seq_len, f32, ct=ct)


def _pool_kernel(x_ref, h_ref, s_ref, o_ref, xc_ref, *, seq_len, n_seq, rt, hist, first_pos):
    hp = _hist_pad(hist)
    scale = s_ref[...]

    def one_seq(s, carry):
        r0 = _fill_window(xc_ref, x_ref, h_ref, s, seq_len, hist)
        for gi, window in enumerate(POOL_WINDOWS):
            @pl.when(pl.program_id(1) == gi)
            def _(window=window):
                def tile(t, c):
                    base = pl.multiple_of(t * rt, SUBLANES)
                    win = xc_ref[pl.ds(base, rt + hp), :]
                    tok = win[hp:hp + rt]
                    acc = tok
                    for k in range(1, window):
                        acc = acc + win[hp - k:hp - k + rt]
                    pos = first_pos + base + lax.broadcasted_iota(jnp.int32, tok.shape, 0)
                    cnt = jnp.minimum(window, pos + 1).astype(f32)
                    o_ref[pl.ds(pl.multiple_of(r0 + base, SUBLANES), rt), :] = (
                        (acc / cnt - tok) * scale).astype(o_ref.dtype)
                    return c

                lax.fori_loop(0, seq_len // rt, tile, 0)
        return carry

    lax.fori_loop(0, n_seq, one_seq, 0)


def _multiscale_pool(u, hist, layer, scale, n_batch, seq_len, first_pos):
    gw = u.shape[1] // len(POOL_WINDOWS)
    rt = _pick(seq_len, (64, 32, 16, 8))
    kern = functools.partial(_pool_kernel, rt=rt, hist=hist.shape[2], first_pos=first_pos)
    return _seq_call("pool", kern, u, hist, layer, [scale.reshape(1, -1)], n_batch, seq_len, bf16, ct=gw)


def _seg_scan(x, seg_len, reverse):
    n = x.shape[0]
    pos = lax.broadcasted_iota(jnp.int32, x.shape, 0) % seg_len
    s = 1
    while s < seg_len:
        if reverse:
            x = x + jnp.where(pos < seg_len - s, pltpu.roll(x, n - s, axis=0), 0.0)
        else:
            x = x + jnp.where(pos >= s, pltpu.roll(x, s, axis=0), 0.0)
        s *= 2
    return x


def _ssd_kernel(xs_ref, b_ref, c_ref, dt_ref, z_ref, alog_ref, dx_ref, g_ref, *rest, seq_len, n_seq, carry_state):
    if carry_state:
        y_ref, hout_ref, h_scr = rest
    else:
        h0_ref, y_ref, hout_ref = rest
    grp = pl.program_id(1)
    width = xs_ref.shape[1]
    hpg = width // SSD_HEADDIM
    rows = xs_ref.shape[0]
    nt = (((1,), (1,)), ((), ()))
    tn = (((0,), (0,)), ((), ()))

    xs = xs_ref[...]
    bm = b_ref[...].astype(bf16)
    cm = c_ref[...].astype(bf16)

    to_front = (LANES - grp * hpg) % LANES
    dt_all = dt_ref[...]
    dt = pltpu.roll(dt_all, to_front, axis=1)
    da = pltpu.roll(dt_all * (-jnp.exp(alog_ref[...])), to_front, axis=1)
    a_cum = _seg_scan(da, seq_len, reverse=False)
    to_end = _seg_scan(da, seq_len, reverse=True) - da

    lane = lax.broadcasted_iota(jnp.int32, (rows, LANES), 1)
    column = lambda v, r: jnp.broadcast_to(v[:, r:r + 1], (rows, LANES))
    heads_per_tile = LANES // SSD_HEADDIM

    def spread(v):
        tiles = []
        for k in range(width // LANES):
            t = column(v, k * heads_per_tile)
            for p in range(1, heads_per_tile):
                t = jnp.where(lane >= p * SSD_HEADDIM, column(v, k * heads_per_tile + p), t)
            tiles.append(t)
        return jnp.concatenate(tiles, axis=1)

    dtx = spread(dt)
    a_cum_x = spread(a_cum)
    to_end_x = spread(to_end)

    xdt = xs * dtx
    xdt_b = xdt.astype(bf16)

    a_cum_t = a_cum.T
    cb = lax.dot_general(cm, bm, nt, preferred_element_type=f32)
    ri = lax.broadcasted_iota(jnp.int32, (rows, rows), 0)
    ci = lax.broadcasted_iota(jnp.int32, (rows, rows), 1)
    allowed = (ri >= ci) & ((ri // seq_len) == (ci // seq_len))
    col_head = lax.broadcasted_iota(jnp.int32, (rows, width), 1) // SSD_HEADDIM
    y = jnp.zeros((rows, width), f32)
    for r in range(hpg):
        a_col = column(a_cum, r)
        a_row = a_cum_t[r:r + 1, :]
        seg = jnp.exp(jnp.where(allowed, a_col - a_row, -jnp.inf))
        m_r = (cb * seg).astype(bf16)
        y_r = jnp.dot(m_r, xdt_b, preferred_element_type=f32)
        y = y + jnp.where(col_head == r, y_r, 0.0)

    xdt_end = (xdt * jnp.exp(to_end_x)).astype(bf16)
    total_x = a_cum_x + to_end_x
    if carry_state:
        chunk = pl.program_id(2)

        @pl.when(chunk == 0)
        def _():
            h_scr[...] = jnp.zeros_like(h_scr)

        h_t = h_scr[...]
        y_off = jnp.dot(cm, h_t.astype(bf16), preferred_element_type=f32)
        st_t = lax.dot_general(bm, xdt_end, tn, preferred_element_type=f32)
        h_new = h_t * jnp.exp(total_x[0:1, :]) + st_t
        h_scr[...] = h_new

        @pl.when(chunk == pl.num_programs(2) - 1)
        def _():
            hout_ref[...] = h_new.T.reshape(hout_ref.shape)
    else:
        h_prev = h0_ref[...].reshape(n_seq * width, SSD_STATE)
        y_all = lax.dot_general(cm, h_prev.astype(bf16), nt, preferred_element_type=f32)
        decay_t = jnp.exp(total_x).T
        row_seq = lax.broadcasted_iota(jnp.int32, (rows, 1), 0) // seq_len
        y_off = jnp.zeros((rows, width), f32)
        for s in range(n_seq):
            in_seq = row_seq == s
            y_off = y_off + jnp.where(in_seq, y_all[:, s * width:(s + 1) * width], 0.0)
            st = lax.dot_general(jnp.where(in_seq, xdt_end, jnp.zeros_like(xdt_end)), bm, tn,
                                 preferred_element_type=f32)
            decay = jnp.broadcast_to(decay_t[:, s * seq_len:s * seq_len + 1], (width, SSD_STATE))
            h_new = h_prev[s * width:(s + 1) * width] * decay + st
            hout_ref[s] = h_new.reshape(hout_ref.shape[1:])
    y = y + y_off * jnp.exp(a_cum_x)

    y = (y + xs * dx_ref[...]) * z_ref[...]
    y = y * lax.rsqrt(jnp.mean(y * y, axis=-1, keepdims=True) + NORM_EPS) * g_ref[...]
    y_ref[...] = y.astype(y_ref.dtype)


def _ssd(xbc, dt, z_act, a_log, d_skip, norm_g, h0, layer, n_batch, seq_len):
    t = xbc.shape[0]
    n_heads = a_log.shape[0]
    inner = n_heads * SSD_HEADDIM
    width = inner // SSD_GROUPS
    hpg = n_heads // SSD_GROUPS
    carry = seq_len >= ROWS
    n_seq = 1 if carry else ROWS // seq_len
    n_chunks = seq_len // ROWS if carry else 1
    n_bblk = n_batch if carry else n_batch // n_seq
    alog_pad = jnp.zeros((1, LANES), f32).at[0, :n_heads].set(a_log)
    b_blk0 = inner // SSD_STATE
    c_blk0 = b_blk0 + SSD_GROUPS
    row = lambda b, g, c: b * n_chunks + c
    state_blk = (n_seq, hpg, SSD_HEADDIM, SSD_STATE)
    in_specs = [
        pl.BlockSpec((ROWS, width), lambda b, g, c: (row(b, g, c), g)),
        pl.BlockSpec((ROWS, SSD_STATE), lambda b, g, c: (row(b, g, c), b_blk0 + g)),
        pl.BlockSpec((ROWS, SSD_STATE), lambda b, g, c: (row(b, g, c), c_blk0 + g)),
        pl.BlockSpec((ROWS, LANES), lambda b, g, c: (row(b, g, c), 0)),
        pl.BlockSpec((ROWS, width), lambda b, g, c: (row(b, g, c), g)),
        pl.BlockSpec((1, LANES), lambda b, g, c: (0, 0)),
        pl.BlockSpec((1, width), lambda b, g, c: (0, g)),
        pl.BlockSpec((1, width), lambda b, g, c: (0, g)),
    ]
    args = [xbc, xbc, xbc, dt, z_act, alog_pad, jnp.repeat(d_skip, SSD_HEADDIM).reshape(1, inner),
            norm_g.reshape(1, inner)]
    scratch = []
    if carry:
        assert h0 is None
        scratch = [pltpu.VMEM((SSD_STATE, width), f32)]
    else:
        in_specs.append(pl.BlockSpec((None,) + state_blk, lambda b, g, c: (layer, b, g, 0, 0)))
        args.append(h0)
    kern = functools.partial(_ssd_kernel, seq_len=ROWS if carry else seq_len, n_seq=n_seq, carry_state=carry)
    y, h_final = pl.pallas_call(
        kern,
        out_shape=[jax.ShapeDtypeStruct((t, inner), bf16),
                   jax.ShapeDtypeStruct((n_batch, n_heads, SSD_HEADDIM, SSD_STATE), f32)],
        grid=(n_bblk, SSD_GROUPS, n_chunks),
        in_specs=in_specs,
        out_specs=[pl.BlockSpec((ROWS, width), lambda b, g, c: (row(b, g, c), g)),
                   pl.BlockSpec(state_blk, lambda b, g, c: (b, g, 0, 0))],
        scratch_shapes=scratch,
        compiler_params=_params("parallel", "parallel", "arbitrary"),
        name="ssd",
    )(*args)
    return y, h_final


def _gmlp_kernel(h_ref, g_ref, b_ref, ws_ref, bias_ref, o_ref, v_ref):
    half = h_ref.shape[1] // 2
    u = h_ref[:, :half]
    v = h_ref[:, half:]
    mu = jnp.mean(v, axis=-1, keepdims=True)
    vc = v - mu
    var = jnp.mean(vc * vc, axis=-1, keepdims=True)
    v = vc * lax.rsqrt(var + NORM_EPS) * g_ref[...] + b_ref[...]
    v_ref[...] = v
    vb = v.astype(bf16)
    gd = half // GMLP_GROUPS
    for g in range(GMLP_GROUPS):
        cs = slice(g * gd, (g + 1) * gd)
        mixed = jnp.dot(ws_ref[g], vb[:, cs], preferred_element_type=f32) + bias_ref[:, cs]
        o_ref[:, cs] = (u[:, cs] * mixed).astype(o_ref.dtype)


def _gmlp(h, ln_g, ln_b, ws, bs, seq_len):
    t, two_w = h.shape
    half = two_w // 2
    q = ROWS if seq_len % ROWS == 0 else seq_len
    reps = ROWS // q
    tri = jnp.tril(jnp.ones((q, q), f32))
    ws_q = ws[:, :q, :q] * tri
    ws_blk = jnp.einsum('ab,gij->gaibj', jnp.eye(reps, dtype=f32), ws_q).reshape(GMLP_GROUPS, ROWS, ROWS).astype(bf16)
    bias = jnp.repeat(jnp.tile(bs[:, :q], (1, reps)).T, half // GMLP_GROUPS, axis=1)
    out, v = pl.pallas_call(
        _gmlp_kernel,
        out_shape=[jax.ShapeDtypeStruct((t, half), bf16), jax.ShapeDtypeStruct((t, half), f32)],
        grid=(t // ROWS,),
        in_specs=[pl.BlockSpec((ROWS, two_w), lambda i: (i, 0)),
                  pl.BlockSpec((1, half), lambda i: (0, 0)),
                  pl.BlockSpec((1, half), lambda i: (0, 0)),
                  pl.BlockSpec((GMLP_GROUPS, ROWS, ROWS), lambda i: (0, 0, 0)),
                  pl.BlockSpec((ROWS, half), lambda i: (0, 0))],
        out_specs=[pl.BlockSpec((ROWS, half), lambda i: (i, 0)), pl.BlockSpec((ROWS, half), lambda i: (i, 0))],
        compiler_params=_params("parallel"),
        name="gmlp",
    )(h, ln_g.reshape(1, half), ln_b.reshape(1, half), ws_blk, bias)
    return out, v


def _merge_kernel(a0, a1, a2, a3, w0, w1, w2, w3, g0, g1, g2, g3, o_ref):
    acc = g0[...].astype(f32) * jnp.dot(a0[...], w0[...], preferred_element_type=f32)
    acc += g1[...].astype(f32) * jnp.dot(a1[...], w1[...], preferred_element_type=f32)
    acc += g2[...].astype(f32) * jnp.dot(a2[...], w2[...], preferred_element_type=f32)
    acc += g3[...].astype(f32) * jnp.dot(a3[...], w3[...], preferred_element_type=f32)
    o_ref[...] = acc.astype(o_ref.dtype)


def _merge(layer, a_ssd, a_pool, a_gmlp, a_conf, w_ssd, w_pool, w_gmlp, w_conf, gates, d_model):
    t = a_ssd.shape[0]
    tm = _pick(t, (1024, 512, 256, 128))
    tn = 512
    grp_in = w_pool.shape[2]
    per_grp = w_pool.shape[3] // tn
    nb = d_model // tn
    full = lambda a: pl.BlockSpec((tm, a.shape[1]), lambda j, i: (i, 0))
    wcol = lambda w: pl.BlockSpec((None, w.shape[1], tn), lambda j, i: (layer, 0, j))
    gate = lambda k: pl.BlockSpec((tm, tn), lambda j, i, k=k: (i, k * nb + j))
    return pl.pallas_call(
        _merge_kernel,
        out_shape=jax.ShapeDtypeStruct((t, d_model), bf16),
        grid=(nb, t // tm),
        in_specs=[full(a_ssd),
                  pl.BlockSpec((tm, grp_in), lambda j, i: (i, j // per_grp)),
                  full(a_gmlp), full(a_conf),
                  wcol(w_ssd),
                  pl.BlockSpec((None, None, grp_in, tn), lambda j, i: (layer, j // per_grp, 0, j % per_grp)),
                  wcol(w_gmlp), wcol(w_conf),
                  gate(0), gate(1), gate(2), gate(3)],
        out_specs=pl.BlockSpec((tm, tn), lambda j, i: (i, j)),
        compiler_params=_params("parallel", "arbitrary"),
        name="merge",
    )(a_ssd, a_pool, a_gmlp, a_conf, w_ssd, w_pool, w_gmlp, w_conf, gates, gates, gates, gates)


def _top_values(work, out_ref, count):
    for r in range(count):
        m = jnp.max(work, axis=0, keepdims=True)
        out_ref[pl.ds(r, 1), :] = m
        work = jnp.where(work == m, -jnp.inf, work)


def _candidate_rows(k):
    return [(i, k // (i + 1)) for i in range(k)]


def _peer_gate_kernel(q_ref, k_ref, w_ref, top1, top2, cand, thr_ref, s2_ref, a_ref, b_ref):
    k = PEER_TOPK
    cand[...] = jnp.full(cand.shape, -jnp.inf, f32)
    for h in range(PEER_HEADS):
        halves = []
        for c in range(2):
            qb = q_ref[:, (2 * h + c) * LANES:(2 * h + c + 1) * LANES]
            halves.append(lax.dot_general(k_ref[h, c], qb, (((1,), (1,)), ((), ())), preferred_element_type=f32))
        s1, s2 = halves
        _top_values(s1, top1, k)
        _top_values(s2, top2, k)
        off = 0
        for i, n_j in _candidate_rows(k):
            cand[pl.ds(off, n_j), :] = top1[pl.ds(i, 1), :] + top2[pl.ds(0, n_j), :]
            off += n_j
        cv = cand[...]
        work = cv
        for r in range(k):
            tau = jnp.max(work, axis=0, keepdims=True)
            work = jnp.where(work == tau, -jnp.inf, work)
        m1 = top1[pl.ds(0, 1), :]
        m2 = top2[pl.ds(0, 1), :]
        z = jnp.sum(jnp.where(cv >= tau, jnp.exp(cv - (m1 + m2)), 0.0), axis=0, keepdims=True)
        thr_ref[h] = (tau - s1) - THRESHOLD_SLACK * (jnp.abs(tau) + jnp.abs(s1))
        s2_ref[h] = s2
        a_ref[h] = jnp.exp(s1 - m1) / z
        b_ref[h] = jnp.exp(s2 - m2)

    def gate_rows(e1, carry):
        w = jnp.zeros((PEER_KEYS, w_ref.shape[1]), f32)
        for h in range(PEER_HEADS):
            hit = s2_ref[h] >= thr_ref[h, pl.ds(e1, 1), :]
            w = w + jnp.where(hit, a_ref[h, pl.ds(e1, 1), :] * b_ref[h], 0.0)
        w_ref[pl.ds(pl.multiple_of(e1 * PEER_KEYS, PEER_KEYS), PEER_KEYS), :] = w.astype(w_ref.dtype)
        return carry

    lax.fori_loop(0, PEER_KEYS, gate_rows, 0)


def _peer_gate(layer, q, keys):
    t = q.shape[0]
    tt = _pick(t, (256, 128))
    n_cand = sum(n for _, n in _candidate_rows(PEER_TOPK))
    per_head = pltpu.VMEM((PEER_HEADS, PEER_KEYS, tt), f32)
    return pl.pallas_call(
        _peer_gate_kernel,
        out_shape=jax.ShapeDtypeStruct((PEER_KEYS * PEER_KEYS, t), bf16),
        grid=(t // tt,),
        in_specs=[pl.BlockSpec((tt, q.shape[1]), lambda i: (i, 0)),
                  pl.BlockSpec((None,) + keys.shape[1:], lambda i: (layer, 0, 0, 0, 0))],
        out_specs=pl.BlockSpec((PEER_KEYS * PEER_KEYS, tt), lambda i: (0, i)),
        scratch_shapes=[pltpu.VMEM((PEER_TOPK, tt), f32), pltpu.VMEM((PEER_TOPK, tt), f32),
                        pltpu.VMEM((_hist_pad(n_cand), tt), f32), per_head, per_head, per_head, per_head],
        compiler_params=_params("parallel"),
        name="peer_gate",
    )(q, keys)


def _peer_mix_kernel(x_ref, u_ref, v_ref, g_ref, o_ref, ht_ref, hw_ref, *, et, sub, pieces):
    j = pl.program_id(1)

    @pl.when(j == 0)
    def _():
        o_ref[...] = jnp.zeros_like(o_ref)

    n_chain = et // sub
    tt, d = x_ref.shape
    nt = (((1,), (1,)), ((), ()))
    tn = (((0,), (0,)), ((), ()))
    kc = d // pieces
    nc = d // pieces

    def pre_act(c, p):
        rows, ks = pl.ds(c * sub, sub), pl.ds(p * kc, kc)
        part = lax.dot_general(x_ref[:, ks], u_ref[rows, ks], nt, preferred_element_type=f32).T
        if p == 0:
            ht_ref[rows, :] = part
        else:
            ht_ref[rows, :] += part

    def gate(c, p):
        rp = sub // pieces
        er = pl.ds(c * sub + p * rp, rp)
        hw_ref[er, :] = (_gelu(ht_ref[er, :]) * g_ref[er, :].astype(f32)).astype(bf16)

    def mix(c, p):
        rows, ns = pl.ds(c * sub, sub), pl.ds(p * nc, nc)
        o_ref[:, ns] += lax.dot_general(hw_ref[rows, :], v_ref[rows, ns], tn, preferred_element_type=f32)

    for p in range(pieces):
        pre_act(0, p)
    for c in range(n_chain):
        for p in range(pieces):
            if c + 1 < n_chain:
                pre_act(c + 1, p)
            if c > 0:
                mix(c - 1, p)
            gate(c, p)
    for p in range(pieces):
        mix(n_chain - 1, p)


def _peer_mix(layer, x, u_tab, v_tab, gate):
    t, d = x.shape
    n_exp = u_tab.shape[1]
    tt = _pick(t, (512, 256, 128))
    et, sub = 512, 256
    tab = pl.BlockSpec((None, et, d), lambda i, j: (layer, j, 0))
    return pl.pallas_call(
        functools.partial(_peer_mix_kernel, et=et, sub=sub, pieces=4),
        out_shape=jax.ShapeDtypeStruct((t, d), f32),
        grid=(t // tt, n_exp // et),
        in_specs=[pl.BlockSpec((tt, d), lambda i, j: (i, 0)), tab, tab,
                  pl.BlockSpec((et, tt), lambda i, j: (j, i))],
        out_specs=pl.BlockSpec((tt, d), lambda i, j: (i, 0)),
        scratch_shapes=[pltpu.VMEM((et, tt), f32), pltpu.VMEM((et, tt), bf16)],
        compiler_params=_params("parallel", "arbitrary"),
        name="peer_mix",
    )(x, u_tab, v_tab, gate)


def _cast_kernel(x_ref, o_ref):
    o_ref[...] = x_ref[...].astype(o_ref.dtype)


def _shift_cast_kernel(a_ref, b_ref, o_ref, *, shift):
    o_ref[...] = jnp.concatenate([a_ref[:, shift:], b_ref[:, :shift]], axis=1).astype(o_ref.dtype)


def _split_w_in(w_in, head_cols, tail0):
    depth, k, n = w_in.shape
    tr, tc = _pick(k, (1024, 512, 256)), 512
    base = (tail0 // LANES) * LANES
    shift = tail0 - base
    n_tail = n - tail0
    assert head_cols % LANES == 0 and base % tc == 0 and n_tail % tc == 0 and 0 < shift < LANES
    head = pl.pallas_call(
        _cast_kernel,
        out_shape=jax.ShapeDtypeStruct((depth, k, head_cols), bf16),
        grid=(depth, k // 256),
        in_specs=[pl.BlockSpec((None, 256, head_cols), lambda l, i: (l, i, 0))],
        out_specs=pl.BlockSpec((None, 256, head_cols), lambda l, i: (l, i, 0)),
        compiler_params=_params("parallel", "parallel"),
        name="cast_head",
    )(w_in)
    tail = pl.pallas_call(
        functools.partial(_shift_cast_kernel, shift=shift),
        out_shape=jax.ShapeDtypeStruct((depth, k, n_tail), bf16),
        grid=(depth, k // tr, n_tail // tc),
        in_specs=[pl.BlockSpec((None, tr, tc), lambda l, i, j: (l, i, base // tc + j)),
                  pl.BlockSpec((None, tr, LANES), lambda l, i, j: (l, i, (base + (j + 1) * tc) // LANES))],
        out_specs=pl.BlockSpec((None, tr, tc), lambda l, i, j: (l, i, j)),
        compiler_params=_params("parallel", "parallel", "parallel"),
        name="cast_tail",
    )(w_in, w_in)
    return head, tail


def _prep_weights(w_in, ssd_dt_bias, w_ssd_out, w_pool_out, w_gmlp_out, w_conf_out, w_o,
                  peer_w_q, peer_sub_keys, peer_u, peer_v, ple_w_gate, ple_w_proj, sizes):
    cast = lambda w: w.astype(bf16)
    inner, xbc_w, n_heads = sizes[:3]
    dt_off = inner + xbc_w
    tail0 = dt_off + n_heads
    assert dt_off % LANES == 0 and n_heads <= LANES and all(s % LANES == 0 for s in sizes[3:])
    depth = w_in.shape[0]
    dt_bias = jnp.zeros((depth, 1, LANES), f32).at[:, 0, :n_heads].set(ssd_dt_bias)
    dt_mask = (jnp.arange(LANES) < n_heads).astype(f32).reshape(1, LANES)
    head, tail = _split_w_in(w_in, dt_off + LANES, tail0)
    return dict(head=head, tail=tail, dt_off=dt_off,
                dt_bias=dt_bias, dt_mask=dt_mask,
                w_ssd_out=cast(w_ssd_out), w_pool_out=cast(w_pool_out), w_gmlp_out=cast(w_gmlp_out),
                w_conf_out=cast(w_conf_out), w_o=cast(w_o), w_q=cast(peer_w_q), keys=cast(peer_sub_keys),
                u=cast(peer_u), v=cast(peer_v), w_ple_gate=cast(ple_w_gate), w_ple_proj=cast(ple_w_proj))


def _tail_rows(hist, x3, keep):
    seq_len = x3.shape[1]
    if seq_len >= keep:
        return x3[:, seq_len - keep:]
    return jnp.concatenate([hist[:, hist.shape[1] - (keep - seq_len):], x3], axis=1)


def _decoder_layer(layer, x, xb, p_b, conv_hist, h0, pool_hist, conf_hist, hist_layer, first_pos, n_batch, seq_len,
                   wts, lp, alpha):
    d_model = x.shape[1]
    inner = lp['ssd_norm_g'].shape[0]
    xbc_w = lp['ssd_conv_w'].shape[1]
    pool_w = lp['pool_scale'].shape[0]
    gmlp_w = lp['gmlp_ln_g'].shape[0]
    conf_w = lp['conf_w'].shape[1]
    head, tail = wts['head'], wts['tail']
    gmlp_off = pool_w
    conf_off = gmlp_off + 2 * gmlp_w
    gate_off = conf_off + 2 * conf_w

    ident = lambda acc: acc
    z_act = _mm("proj_z", layer, [xb], [(0, head, 0)], inner, lambda acc: _silu(acc))
    xbc_raw = _mm("proj_xbc", layer, [xb], [(0, head, inner)], xbc_w, ident)
    dt = _mm("proj_dt", layer, [xb], [(0, head, wts['dt_off'])], LANES,
             lambda acc, bias, mask: jnp.where(mask > 0.0, _softplus(acc + bias), 0.0),
             extras=[(wts['dt_bias'][layer], 'row', 0), (wts['dt_mask'], 'row', 0)])
    pool_u = _mm("proj_pool", layer, [xb], [(0, tail, 0)], pool_w, ident)
    gmlp_h = _mm("proj_gmlp", layer, [xb], [(0, tail, gmlp_off)], 2 * gmlp_w, lambda acc: _gelu(acc))
    glu = _mm("proj_glu", layer, [xb], [(0, tail, conf_off), (0, tail, conf_off + conf_w)], conf_w,
              lambda ca, cg: ca * _sigmoid(cg))
    gates = _mm("proj_gates", layer, [xb], [(0, tail, gate_off)], N_BRANCH * d_model, lambda acc: _sigmoid(acc),
                out_dtype=bf16)

    xbc_act = _causal_conv(xbc_raw, conv_hist, hist_layer, lp['ssd_conv_w'], lp['ssd_conv_b'], n_batch, seq_len,
                           act='silu')
    y_ssd, h_new = _ssd(xbc_act, dt, z_act, lp['ssd_a_log'], lp['ssd_d'], lp['ssd_norm_g'], h0, hist_layer,
                        n_batch, seq_len)
    conv_new = _tail_rows(conv_hist[hist_layer], xbc_raw.reshape(n_batch, seq_len, xbc_w), SSD_CONV - 1)

    pooled = _multiscale_pool(pool_u, pool_hist, hist_layer, lp['pool_scale'], n_batch, seq_len, first_pos)
    pool_new = _tail_rows(pool_hist[hist_layer], pool_u.reshape(n_batch, seq_len, pool_w), pool_hist.shape[2])

    a_gmlp, v_gmlp = _gmlp(gmlp_h, lp['gmlp_ln_g'], lp['gmlp_ln_b'], lp['gmlp_ws'], lp['gmlp_bs'], seq_len)

    cconv = _causal_conv(glu, conf_hist, hist_layer, lp['conf_w'], lp['conf_b'], n_batch, seq_len, act=None)
    a_conf, = _layer_norm(cconv, lp['conf_ln_g'], lp['conf_ln_b'], [bf16], act='silu')
    conf_new = _tail_rows(conf_hist[hist_layer], glu.reshape(n_batch, seq_len, conf_w), CONF_CONV - 1)

    merged = _merge(layer, y_ssd, pooled, a_gmlp, a_conf, wts['w_ssd_out'], wts['w_pool_out'], wts['w_gmlp_out'],
                    wts['w_conf_out'], gates, d_model)
    pre1 = _mm("out_proj", layer, [merged], [(0, wts['w_o'], 0)], d_model, lambda acc, res: alpha * res + acc,
               extras=[(x, 'tile', 0)])
    x1, x1b = _layer_norm(pre1, lp['ln_mix_g'], lp['ln_mix_b'], [f32, bf16])

    q = _mm("peer_q", layer, [x1b], [(0, wts['w_q'], 0)], wts['w_q'].shape[2], ident, out_dtype=bf16)
    gate = _peer_gate(layer, q, wts['keys'])
    ffn = _peer_mix(layer, x1b, wts['u'], wts['v'], gate)
    pre2 = _mm("ple_ffn", layer, [x1b, p_b], [(0, wts['w_ple_gate'], 0), (1, wts['w_ple_proj'], 0)], d_model,
               lambda g, pr, res, f: alpha * res + f + _sigmoid(g) * pr,
               extras=[(x1, 'tile', 0), (ffn, 'tile', 0)])
    x2, x2b = _layer_norm(pre2, lp['ln_ffn_g'], lp['ln_ffn_b'], [f32, bf16])
    return x2, x2b, conv_new, h_new, pool_new, conf_new, v_gmlp


def kernel(x_prompt, x_sample, p_prompt, p_sample, state_ssd_conv, state_ssd_ssm, state_pool, state_conf, w_in, ssd_conv_w, ssd_conv_b, ssd_dt_bias, ssd_a_log, ssd_d, ssd_norm_g, w_ssd_out, pool_scale, w_pool_out, gmlp_ln_g, gmlp_ln_b, gmlp_ws, gmlp_bs, w_gmlp_out, conf_w, conf_b, conf_ln_g, conf_ln_b, w_conf_out, w_o, ln_mix_g, ln_mix_b, peer_w_q, peer_sub_keys, peer_u, peer_v, ple_w_gate, ple_w_proj, ln_ffn_g, ln_ffn_b):
    depth = w_in.shape[0]
    n_b, seq, d_model = x_prompt.shape
    n_db, dseq, _ = x_sample.shape
    alpha = (2.0 * depth) ** 0.25
    n_heads = ssd_a_log.shape[1]
    inner = ssd_norm_g.shape[1]
    xbc_w = ssd_conv_w.shape[2]
    pool_w = pool_scale.shape[1]
    gmlp_w = gmlp_ln_g.shape[1]
    conf_width = conf_w.shape[2]
    sizes = (inner, xbc_w, n_heads, pool_w, 2 * gmlp_w, 2 * conf_width, N_BRANCH * d_model)
    wts = _prep_weights(w_in, ssd_dt_bias, w_ssd_out, w_pool_out, w_gmlp_out, w_conf_out, w_o,
                        peer_w_q, peer_sub_keys, peer_u, peer_v, ple_w_gate, ple_w_proj, sizes)

    yp = x_prompt.reshape(n_b * seq, d_model)
    ys = x_sample.reshape(n_db * dseq, d_model)
    ypb, ysb = yp.astype(bf16), ys.astype(bf16)
    zeros = lambda h, c: jnp.zeros((1, n_b, h, c), f32)
    outs_p, outs_s = [], []
    for i in range(depth):
        lp = dict(ssd_conv_w=ssd_conv_w[i], ssd_conv_b=ssd_conv_b[i], ssd_a_log=ssd_a_log[i], ssd_d=ssd_d[i],
                  ssd_norm_g=ssd_norm_g[i], pool_scale=pool_scale[i], gmlp_ln_g=gmlp_ln_g[i], gmlp_ln_b=gmlp_ln_b[i],
                  gmlp_ws=gmlp_ws[i], gmlp_bs=gmlp_bs[i], conf_w=conf_w[i], conf_b=conf_b[i],
                  conf_ln_g=conf_ln_g[i], conf_ln_b=conf_ln_b[i], ln_mix_g=ln_mix_g[i], ln_mix_b=ln_mix_b[i],
                  ln_ffn_g=ln_ffn_g[i], ln_ffn_b=ln_ffn_b[i])
        yp, ypb, *st_p = _decoder_layer(
            i, yp, ypb, p_prompt[i].reshape(n_b * seq, -1).astype(bf16),
            zeros(SSD_CONV - 1, xbc_w), None, zeros(max(POOL_WINDOWS) - 1, pool_w), zeros(CONF_CONV - 1, conf_width),
            0, 0, n_b, seq, wts, lp, alpha)
        ys, ysb, *st_s = _decoder_layer(
            i, ys, ysb, p_sample[i].reshape(n_db * dseq, -1).astype(bf16),
            state_ssd_conv, state_ssd_ssm, state_pool, state_conf,
            i, PAST_LEN, n_db, dseq, wts, lp, alpha)
        outs_p.append(st_p)
        outs_s.append(st_s)
    stack = lambda outs, k: jnp.stack([o[k] for o in outs])
    return (yp.reshape(n_b, seq, d_model), ys.reshape(n_db, dseq, d_model),
            stack(outs_p, 0), stack(outs_p, 1), stack(outs_p, 2), stack(outs_p, 3),
            stack(outs_s, 0), stack(outs_s, 1), stack(outs_s, 2), stack(outs_s, 3),
            stack(outs_s, 4).reshape(depth, n_db, dseq, gmlp_w))
```

```python
import functools
import math

import jax
import jax.numpy as jnp
from jax import lax
from jax.experimental import pallas as pl
from jax.experimental.pallas import tpu as pltpu

f32 = jnp.float32
bf16 = jnp.bfloat16

V7X_VMEM_BYTES = 64 * 1024 * 1024
VMEM_LIMIT = V7X_VMEM_BYTES - 8 * 1024 * 1024
LANES = 128
SUBLANES = 8

NORM_EPS = 1e-5
SSD_HEADDIM = 64
SSD_GROUPS = 4
SSD_STATE = 128
SSD_CONV = 4
POOL_WINDOWS = (2, 4, 8, 16)
GMLP_GROUPS = 8
CONF_CONV = 31
N_BRANCH = 4
PEER_KEYS = 128
PEER_HEADS = 8
PEER_TOPK = 16
PAST_LEN = 16384
ROWS = 128
THRESHOLD_SLACK = 2.0 ** -22


def _params(*sem):
    return pltpu.CompilerParams(dimension_semantics=sem, vmem_limit_bytes=VMEM_LIMIT)


def _pick(n, prefs):
    for p in prefs:
        if n % p == 0:
            return p
    return n


def _sigmoid(x):
    return 1.0 / (1.0 + jnp.exp(-x))


def _silu(x):
    return x * _sigmoid(x)


def _gelu(x):
    return 0.5 * x * (1.0 + lax.erf(x * (1.0 / math.sqrt(2.0))))


def _softplus(x):
    return jnp.maximum(x, 0.0) + jnp.log(1.0 + jnp.exp(-jnp.abs(x)))


def _mm_kernel(*refs, n_a, w_src, n_extra, epi):
    a_refs = refs[:n_a]
    w_refs = refs[n_a:n_a + len(w_src)]
    e_refs = refs[n_a + len(w_src):n_a + len(w_src) + n_extra]
    o_ref = refs[-1]
    a_vals = [r[...] for r in a_refs]
    accs = [jnp.dot(a_vals[s], w[...], preferred_element_type=f32) for s, w in zip(w_src, w_refs)]
    o_ref[...] = epi(*accs, *[e[...] for e in e_refs]).astype(o_ref.dtype)


def _mm(name, layer, a_list, w_list, n_out, epi, extras=(), out_dtype=f32, tm=None, tn=None):
    m = a_list[0].shape[0]
    tm = tm or _pick(m, (1024, 512, 256, 128))
    tn = tn or _pick(n_out, (512, 256, 128))
    in_specs, args = [], []
    for a in a_list:
        in_specs.append(pl.BlockSpec((tm, a.shape[1]), lambda j, i: (i, 0)))
        args.append(a)
    for (_, w, off) in w_list:
        ob = off // tn
        in_specs.append(pl.BlockSpec((None, w.shape[1], tn), lambda j, i, ob=ob: (layer, 0, j + ob)))
        args.append(w)
    for (e, kind, off) in extras:
        ob = off // tn
        if kind == 'tile':
            in_specs.append(pl.BlockSpec((tm, tn), lambda j, i, ob=ob: (i, j + ob)))
        else:
            in_specs.append(pl.BlockSpec((1, tn), lambda j, i, ob=ob: (0, j + ob)))
        args.append(e)
    kern = functools.partial(_mm_kernel, n_a=len(a_list), w_src=tuple(s for s, _, _ in w_list),
                             n_extra=len(extras), epi=epi)
    return pl.pallas_call(
        kern,
        out_shape=jax.ShapeDtypeStruct((m, n_out), out_dtype),
        grid=(n_out // tn, m // tm),
        in_specs=in_specs,
        out_specs=pl.BlockSpec((tm, tn), lambda j, i: (i, j)),
        compiler_params=_params("parallel", "arbitrary"),
        name=name,
    )(*args)


def _ln_kernel(x_ref, g_ref, b_ref, *o_refs, act):
    x = x_ref[...]
    mu = jnp.mean(x, axis=-1, keepdims=True)
    xc = x - mu
    var = jnp.mean(xc * xc, axis=-1, keepdims=True)
    y = xc * lax.rsqrt(var + NORM_EPS) * g_ref[...] + b_ref[...]
    if act == 'silu':
        y = _silu(y)
    for o in o_refs:
        o[...] = y.astype(o.dtype)


def _layer_norm(x, g, b, out_dtypes, act=None):
    m, n = x.shape
    tm = _pick(m, (256, 128))
    outs = pl.pallas_call(
        functools.partial(_ln_kernel, act=act),
        out_shape=[jax.ShapeDtypeStruct((m, n), d) for d in out_dtypes],
        grid=(m // tm,),
        in_specs=[pl.BlockSpec((tm, n), lambda i: (i, 0)),
                  pl.BlockSpec((1, n), lambda i: (0, 0)),
                  pl.BlockSpec((1, n), lambda i: (0, 0))],
        out_specs=[pl.BlockSpec((tm, n), lambda i: (i, 0)) for _ in out_dtypes],
        compiler_params=_params("parallel"),
        name="layer_norm",
    )(x, g.reshape(1, n), b.reshape(1, n))
    return outs


def _hist_pad(h):
    return -(-h // SUBLANES) * SUBLANES


def _fill_window(xc_ref, x_ref, h_ref, s, seq_len, hist):
    hp = _hist_pad(hist)
    r0 = pl.multiple_of(s * seq_len, SUBLANES)
    xc_ref[pl.ds(hp - hist, hist), :] = h_ref[s]
    xc_ref[pl.ds(hp, seq_len), :] = x_ref[pl.ds(r0, seq_len), :]
    return r0


def _conv_kernel(x_ref, h_ref, w_ref, b_ref, o_ref, xc_ref, *, seq_len, width, n_seq, rt, act):
    hist = width - 1
    hp = _hist_pad(hist)
    w = w_ref[...]
    bias = b_ref[...]

    def one_seq(s, carry):
        r0 = _fill_window(xc_ref, x_ref, h_ref, s, seq_len, hist)

        def tile(t, c):
            base = pl.multiple_of(t * rt, SUBLANES)
            win = xc_ref[pl.ds(base, rt + hp), :]
            acc = jnp.broadcast_to(bias, (rt, bias.shape[1]))
            rotated = {0: win}
            for k in range(width):
                base8, s = divmod(hp - hist + k, SUBLANES)
                if s not in rotated:
                    rotated[s] = pltpu.roll(win, win.shape[0] - s, axis=0)
                acc = acc + rotated[s][base8 * SUBLANES:base8 * SUBLANES + rt] * w[k:k + 1]
            if act == 'silu':
                acc = _silu(acc)
            o_ref[pl.ds(pl.multiple_of(r0 + base, SUBLANES), rt), :] = acc
            return c

        lax.fori_loop(0, seq_len // rt, tile, 0)
        return carry

    lax.fori_loop(0, n_seq, one_seq, 0)


def _seq_call(name, kernel_fn, x, hist, layer, small, n_batch, seq_len, out_dtype, ct):
    c = x.shape[1]
    hist_len = hist.shape[2]
    n_seq = 1 if seq_len >= ROWS else _pick(n_batch, (16, 8, 4, 2, 1))
    rows = n_seq * seq_len
    hp = _hist_pad(hist_len)
    in_specs = [pl.BlockSpec((rows, ct), lambda b, j: (b, j)),
                pl.BlockSpec((None, n_seq, hist_len, ct), lambda b, j: (layer, b, 0, j))]
    in_specs += [pl.BlockSpec((p.shape[0], ct), lambda b, j: (0, j)) for p in small]
    return pl.pallas_call(
        functools.partial(kernel_fn, seq_len=seq_len, n_seq=n_seq),
        out_shape=jax.ShapeDtypeStruct(x.shape, out_dtype),
        grid=(n_batch // n_seq, c // ct),
        in_specs=in_specs,
        out_specs=pl.BlockSpec((rows, ct), lambda b, j: (b, j)),
        scratch_shapes=[pltpu.VMEM((hp + seq_len, ct), f32)],
        compiler_params=_params("parallel", "parallel"),
        name=name,
    )(x, hist, *small)


def _causal_conv(x, hist, layer, w, b, n_batch, seq_len, act):
    width = w.shape[0]
    rt = _pick(seq_len, (64, 32, 16, 8))
    kern = functools.partial(_conv_kernel, width=width, rt=rt, act=act)
    ct = 256 if seq_len >= ROWS else _pick(x.shape[1], (1024, 512, 256))
    return _seq_call("causal_conv", kern, x, hist, layer, [w, b.reshape(1, -1)], n_batch, seq_len, f32, ct=ct)


def _pool_kernel(x_ref, h_ref, s_ref, o_ref, xc_ref, *, seq_len, n_seq, rt, hist, first_pos):
    hp = _hist_pad(hist)
    scale = s_ref[...]

    def one_seq(s, carry):
        r0 = _fill_window(xc_ref, x_ref, h_ref, s, seq_len, hist)
        for gi, window in enumerate(POOL_WINDOWS):
            @pl.when(pl.program_id(1) == gi)
            def _(window=window):
                def tile(t, c):
                    base = pl.multiple_of(t * rt, SUBLANES)
                    win = xc_ref[pl.ds(base, rt + hp), :]
                    tok = win[hp:hp + rt]
                    acc = tok
                    for k in range(1, window):
                        acc = acc + win[hp - k:hp - k + rt]
                    pos = first_pos + base + lax.broadcasted_iota(jnp.int32, tok.shape, 0)
                    cnt = jnp.minimum(window, pos + 1).astype(f32)
                    o_ref[pl.ds(pl.multiple_of(r0 + base, SUBLANES), rt), :] = (
                        (acc / cnt - tok) * scale).astype(o_ref.dtype)
                    return c

                lax.fori_loop(0, seq_len // rt, tile, 0)
        return carry

    lax.fori_loop(0, n_seq, one_seq, 0)


def _multiscale_pool(u, hist, layer, scale, n_batch, seq_len, first_pos):
    gw = u.shape[1] // len(POOL_WINDOWS)
    rt = _pick(seq_len, (64, 32, 16, 8))
    kern = functools.partial(_pool_kernel, rt=rt, hist=hist.shape[2], first_pos=first_pos)
    return _seq_call("pool", kern, u, hist, layer, [scale.reshape(1, -1)], n_batch, seq_len, bf16, ct=gw)


def _seg_scan(x, seg_len, reverse):
    n = x.shape[0]
    pos = lax.broadcasted_iota(jnp.int32, x.shape, 0) % seg_len
    s = 1
    while s < seg_len:
        if reverse:
            x = x + jnp.where(pos < seg_len - s, pltpu.roll(x, n - s, axis=0), 0.0)
        else:
            x = x + jnp.where(pos >= s, pltpu.roll(x, s, axis=0), 0.0)
        s *= 2
    return x


def _ssd_kernel(xs_ref, b_ref, c_ref, dt_ref, z_ref, alog_ref, dx_ref, g_ref, *rest, seq_len, n_seq, carry_state):
    if carry_state:
        y_ref, hout_ref, h_scr = rest
    else:
        h0_ref, y_ref, hout_ref = rest
    grp = pl.program_id(1)
    width = xs_ref.shape[1]
    hpg = width // SSD_HEADDIM
    rows = xs_ref.shape[0]
    nt = (((1,), (1,)), ((), ()))
    tn = (((0,), (0,)), ((), ()))

    xs = xs_ref[...]
    bm = b_ref[...].astype(bf16)
    cm = c_ref[...].astype(bf16)

    to_front = (LANES - grp * hpg) % LANES
    dt_all = dt_ref[...]
    dt = pltpu.roll(dt_all, to_front, axis=1)
    da = pltpu.roll(dt_all * (-jnp.exp(alog_ref[...])), to_front, axis=1)
    a_cum = _seg_scan(da, seq_len, reverse=False)
    to_end = _seg_scan(da, seq_len, reverse=True) - da

    lane = lax.broadcasted_iota(jnp.int32, (rows, LANES), 1)
    column = lambda v, r: jnp.broadcast_to(v[:, r:r + 1], (rows, LANES))
    heads_per_tile = LANES // SSD_HEADDIM

    def spread(v):
        tiles = []
        for k in range(width // LANES):
            t = column(v, k * heads_per_tile)
            for p in range(1, heads_per_tile):
                t = jnp.where(lane >= p * SSD_HEADDIM, column(v, k * heads_per_tile + p), t)
            tiles.append(t)
        return jnp.concatenate(tiles, axis=1)

    dtx = spread(dt)
    a_cum_x = spread(a_cum)
    to_end_x = spread(to_end)

    xdt = xs * dtx
    xdt_b = xdt.astype(bf16)

    a_cum_t = a_cum.T
    cb = lax.dot_general(cm, bm, nt, preferred_element_type=f32)
    ri = lax.broadcasted_iota(jnp.int32, (rows, rows), 0)
    ci = lax.broadcasted_iota(jnp.int32, (rows, rows), 1)
    allowed = (ri >= ci) & ((ri // seq_len) == (ci // seq_len))
    col_head = lax.broadcasted_iota(jnp.int32, (rows, width), 1) // SSD_HEADDIM
    y = jnp.zeros((rows, width), f32)
    for r in range(hpg):
        a_col = column(a_cum, r)
        a_row = a_cum_t[r:r + 1, :]
        seg = jnp.exp(jnp.where(allowed, a_col - a_row, -jnp.inf))
        m_r = (cb * seg).astype(bf16)
        y_r = jnp.dot(m_r, xdt_b, preferred_element_type=f32)
        y = y + jnp.where(col_head == r, y_r, 0.0)

    xdt_end = (xdt * jnp.exp(to_end_x)).astype(bf16)
    total_x = a_cum_x + to_end_x
    if carry_state:
        chunk = pl.program_id(2)

        @pl.when(chunk == 0)
        def _():
            h_scr[...] = jnp.zeros_like(h_scr)

        h_t = h_scr[...]
        y_off = jnp.dot(cm, h_t.astype(bf16), preferred_element_type=f32)
        st_t = lax.dot_general(bm, xdt_end, tn, preferred_element_type=f32)
        h_new = h_t * jnp.exp(total_x[0:1, :]) + st_t
        h_scr[...] = h_new

        @pl.when(chunk == pl.num_programs(2) - 1)
        def _():
            hout_ref[...] = h_new.T.reshape(hout_ref.shape)
    else:
        h_prev = h0_ref[...].reshape(n_seq * width, SSD_STATE)
        y_all = lax.dot_general(cm, h_prev.astype(bf16), nt, preferred_element_type=f32)
        decay_t = jnp.exp(total_x).T
        row_seq = lax.broadcasted_iota(jnp.int32, (rows, 1), 0) // seq_len
        y_off = jnp.zeros((rows, width), f32)
        for s in range(n_seq):
            in_seq = row_seq == s
            y_off = y_off + jnp.where(in_seq, y_all[:, s * width:(s + 1) * width], 0.0)
            st = lax.dot_general(jnp.where(in_seq, xdt_end, jnp.zeros_like(xdt_end)), bm, tn,
                                 preferred_element_type=f32)
            decay = jnp.broadcast_to(decay_t[:, s * seq_len:s * seq_len + 1], (width, SSD_STATE))
            h_new = h_prev[s * width:(s + 1) * width] * decay + st
            hout_ref[s] = h_new.reshape(hout_ref.shape[1:])
    y = y + y_off * jnp.exp(a_cum_x)

    y = (y + xs * dx_ref[...]) * z_ref[...]
    y = y * lax.rsqrt(jnp.mean(y * y, axis=-1, keepdims=True) + NORM_EPS) * g_ref[...]
    y_ref[...] = y.astype(y_ref.dtype)


def _ssd(xbc, dt, z_act, a_log, d_skip, norm_g, h0, layer, n_batch, seq_len):
    t = xbc.shape[0]
    n_heads = a_log.shape[0]
    inner = n_heads * SSD_HEADDIM
    width = inner // SSD_GROUPS
    hpg = n_heads // SSD_GROUPS
    carry = seq_len >= ROWS
    n_seq = 1 if carry else ROWS // seq_len
    n_chunks = seq_len // ROWS if carry else 1
    n_bblk = n_batch if carry else n_batch // n_seq
    alog_pad = jnp.zeros((1, LANES), f32).at[0, :n_heads].set(a_log)
    b_blk0 = inner // SSD_STATE
    c_blk0 = b_blk0 + SSD_GROUPS
    row = lambda b, g, c: b * n_chunks + c
    state_blk = (n_seq, hpg, SSD_HEADDIM, SSD_STATE)
    in_specs = [
        pl.BlockSpec((ROWS, width), lambda b, g, c: (row(b, g, c), g)),
        pl.BlockSpec((ROWS, SSD_STATE), lambda b, g, c: (row(b, g, c), b_blk0 + g)),
        pl.BlockSpec((ROWS, SSD_STATE), lambda b, g, c: (row(b, g, c), c_blk0 + g)),
        pl.BlockSpec((ROWS, LANES), lambda b, g, c: (row(b, g, c), 0)),
        pl.BlockSpec((ROWS, width), lambda b, g, c: (row(b, g, c), g)),
        pl.BlockSpec((1, LANES), lambda b, g, c: (0, 0)),
        pl.BlockSpec((1, width), lambda b, g, c: (0, g)),
        pl.BlockSpec((1, width), lambda b, g, c: (0, g)),
    ]
    args = [xbc, xbc, xbc, dt, z_act, alog_pad, jnp.repeat(d_skip, SSD_HEADDIM).reshape(1, inner),
            norm_g.reshape(1, inner)]
    scratch = []
    if carry:
        assert h0 is None
        scratch = [pltpu.VMEM((SSD_STATE, width), f32)]
    else:
        in_specs.append(pl.BlockSpec((None,) + state_blk, lambda b, g, c: (layer, b, g, 0, 0)))
        args.append(h0)
    kern = functools.partial(_ssd_kernel, seq_len=ROWS if carry else seq_len, n_seq=n_seq, carry_state=carry)
    y, h_final = pl.pallas_call(
        kern,
        out_shape=[jax.ShapeDtypeStruct((t, inner), bf16),
                   jax.ShapeDtypeStruct((n_batch, n_heads, SSD_HEADDIM, SSD_STATE), f32)],
        grid=(n_bblk, SSD_GROUPS, n_chunks),
        in_specs=in_specs,
        out_specs=[pl.BlockSpec((ROWS, width), lambda b, g, c: (row(b, g, c), g)),
                   pl.BlockSpec(state_blk, lambda b, g, c: (b, g, 0, 0))],
        scratch_shapes=scratch,
        compiler_params=_params("parallel", "parallel", "arbitrary"),
        name="ssd",
    )(*args)
    return y, h_final


def _gmlp_kernel(h_ref, g_ref, b_ref, ws_ref, bias_ref, o_ref, v_ref):
    half = h_ref.shape[1] // 2
    u = h_ref[:, :half]
    v = h_ref[:, half:]
    mu = jnp.mean(v, axis=-1, keepdims=True)
    vc = v - mu
    var = jnp.mean(vc * vc, axis=-1, keepdims=True)
    v = vc * lax.rsqrt(var + NORM_EPS) * g_ref[...] + b_ref[...]
    v_ref[...] = v
    vb = v.astype(bf16)
    gd = half // GMLP_GROUPS
    for g in range(GMLP_GROUPS):
        cs = slice(g * gd, (g + 1) * gd)
        mixed = jnp.dot(ws_ref[g], vb[:, cs], preferred_element_type=f32) + bias_ref[:, cs]
        o_ref[:, cs] = (u[:, cs] * mixed).astype(o_ref.dtype)


def _gmlp(h, ln_g, ln_b, ws, bs, seq_len):
    t, two_w = h.shape
    half = two_w // 2
    q = ROWS if seq_len % ROWS == 0 else seq_len
    reps = ROWS // q
    tri = jnp.tril(jnp.ones((q, q), f32))
    ws_q = ws[:, :q, :q] * tri
    ws_blk = jnp.einsum('ab,gij->gaibj', jnp.eye(reps, dtype=f32), ws_q).reshape(GMLP_GROUPS, ROWS, ROWS).astype(bf16)
    bias = jnp.repeat(jnp.tile(bs[:, :q], (1, reps)).T, half // GMLP_GROUPS, axis=1)
    out, v = pl.pallas_call(
        _gmlp_kernel,
        out_shape=[jax.ShapeDtypeStruct((t, half), bf16), jax.ShapeDtypeStruct((t, half), f32)],
        grid=(t // ROWS,),
        in_specs=[pl.BlockSpec((ROWS, two_w), lambda i: (i, 0)),
                  pl.BlockSpec((1, half), lambda i: (0, 0)),
                  pl.BlockSpec((1, half), lambda i: (0, 0)),
                  pl.BlockSpec((GMLP_GROUPS, ROWS, ROWS), lambda i: (0, 0, 0)),
                  pl.BlockSpec((ROWS, half), lambda i: (0, 0))],
        out_specs=[pl.BlockSpec((ROWS, half), lambda i: (i, 0)), pl.BlockSpec((ROWS, half), lambda i: (i, 0))],
        compiler_params=_params("parallel"),
        name="gmlp",
    )(h, ln_g.reshape(1, half), ln_b.reshape(1, half), ws_blk, bias)
    return out, v


def _merge_kernel(a0, a1, a2, a3, w0, w1, w2, w3, g0, g1, g2, g3, o_ref):
    acc = g0[...].astype(f32) * jnp.dot(a0[...], w0[...], preferred_element_type=f32)
    acc += g1[...].astype(f32) * jnp.dot(a1[...], w1[...], preferred_element_type=f32)
    acc += g2[...].astype(f32) * jnp.dot(a2[...], w2[...], preferred_element_type=f32)
    acc += g3[...].astype(f32) * jnp.dot(a3[...], w3[...], preferred_element_type=f32)
    o_ref[...] = acc.astype(o_ref.dtype)


def _merge(layer, a_ssd, a_pool, a_gmlp, a_conf, w_ssd, w_pool, w_gmlp, w_conf, gates, d_model):
    t = a_ssd.shape[0]
    tm = _pick(t, (1024, 512, 256, 128))
    tn = 512
    grp_in = w_pool.shape[2]
    per_grp = w_pool.shape[3] // tn
    nb = d_model // tn
    full = lambda a: pl.BlockSpec((tm, a.shape[1]), lambda j, i: (i, 0))
    wcol = lambda w: pl.BlockSpec((None, w.shape[1], tn), lambda j, i: (layer, 0, j))
    gate = lambda k: pl.BlockSpec((tm, tn), lambda j, i, k=k: (i, k * nb + j))
    return pl.pallas_call(
        _merge_kernel,
        out_shape=jax.ShapeDtypeStruct((t, d_model), bf16),
        grid=(nb, t // tm),
        in_specs=[full(a_ssd),
                  pl.BlockSpec((tm, grp_in), lambda j, i: (i, j // per_grp)),
                  full(a_gmlp), full(a_conf),
                  wcol(w_ssd),
                  pl.BlockSpec((None, None, grp_in, tn), lambda j, i: (layer, j // per_grp, 0, j % per_grp)),
                  wcol(w_gmlp), wcol(w_conf),
                  gate(0), gate(1), gate(2), gate(3)],
        out_specs=pl.BlockSpec((tm, tn), lambda j, i: (i, j)),
        compiler_params=_params("parallel", "arbitrary"),
        name="merge",
    )(a_ssd, a_pool, a_gmlp, a_conf, w_ssd, w_pool, w_gmlp, w_conf, gates, gates, gates, gates)


def _top_values(work, out_ref, count):
    for r in range(count):
        m = jnp.max(work, axis=0, keepdims=True)
        out_ref[pl.ds(r, 1), :] = m
        work = jnp.where(work == m, -jnp.inf, work)


def _candidate_rows(k):
    return [(i, k // (i + 1)) for i in range(k)]


def _peer_gate_kernel(q_ref, k_ref, w_ref, top1, top2, cand, thr_ref, s2_ref, a_ref, b_ref):
    k = PEER_TOPK
    cand[...] = jnp.full(cand.shape, -jnp.inf, f32)
    for h in range(PEER_HEADS):
        halves = []
        for c in range(2):
            qb = q_ref[:, (2 * h + c) * LANES:(2 * h + c + 1) * LANES]
            halves.append(lax.dot_general(k_ref[h, c], qb, (((1,), (1,)), ((), ())), preferred_element_type=f32))
        s1, s2 = halves
        _top_values(s1, top1, k)
        _top_values(s2, top2, k)
        off = 0
        for i, n_j in _candidate_rows(k):
            cand[pl.ds(off, n_j), :] = top1[pl.ds(i, 1), :] + top2[pl.ds(0, n_j), :]
            off += n_j
        cv = cand[...]
        work = cv
        for r in range(k):
            tau = jnp.max(work, axis=0, keepdims=True)
            work = jnp.where(work == tau, -jnp.inf, work)
        m1 = top1[pl.ds(0, 1), :]
        m2 = top2[pl.ds(0, 1), :]
        z = jnp.sum(jnp.where(cv >= tau, jnp.exp(cv - (m1 + m2)), 0.0), axis=0, keepdims=True)
        thr_ref[h] = (tau - s1) - THRESHOLD_SLACK * (jnp.abs(tau) + jnp.abs(s1))
        s2_ref[h] = s2
        a_ref[h] = jnp.exp(s1 - m1) / z
        b_ref[h] = jnp.exp(s2 - m2)

    def gate_rows(e1, carry):
        w = jnp.zeros((PEER_KEYS, w_ref.shape[1]), f32)
        for h in range(PEER_HEADS):
            hit = s2_ref[h] >= thr_ref[h, pl.ds(e1, 1), :]
            w = w + jnp.where(hit, a_ref[h, pl.ds(e1, 1), :] * b_ref[h], 0.0)
        w_ref[pl.ds(pl.multiple_of(e1 * PEER_KEYS, PEER_KEYS), PEER_KEYS), :] = w.astype(w_ref.dtype)
        return carry

    lax.fori_loop(0, PEER_KEYS, gate_rows, 0)


def _peer_gate(layer, q, keys):
    t = q.shape[0]
    tt = _pick(t, (256, 128))
    n_cand = sum(n for _, n in _candidate_rows(PEER_TOPK))
    per_head = pltpu.VMEM((PEER_HEADS, PEER_KEYS, tt), f32)
    return pl.pallas_call(
        _peer_gate_kernel,
        out_shape=jax.ShapeDtypeStruct((PEER_KEYS * PEER_KEYS, t), bf16),
        grid=(t // tt,),
        in_specs=[pl.BlockSpec((tt, q.shape[1]), lambda i: (i, 0)),
                  pl.BlockSpec((None,) + keys.shape[1:], lambda i: (layer, 0, 0, 0, 0))],
        out_specs=pl.BlockSpec((PEER_KEYS * PEER_KEYS, tt), lambda i: (0, i)),
        scratch_shapes=[pltpu.VMEM((PEER_TOPK, tt), f32), pltpu.VMEM((PEER_TOPK, tt), f32),
                        pltpu.VMEM((_hist_pad(n_cand), tt), f32), per_head, per_head, per_head, per_head],
        compiler_params=_params("parallel"),
        name="peer_gate",
    )(q, keys)


def _peer_mix_kernel(x_ref, u_ref, v_ref, g_ref, o_ref, ht_ref, hw_ref, *, et, sub, pieces):
    j = pl.program_id(1)

    @pl.when(j == 0)
    def _():
        o_ref[...] = jnp.zeros_like(o_ref)

    n_chain = et // sub
    tt, d = x_ref.shape
    nt = (((1,), (1,)), ((), ()))
    tn = (((0,), (0,)), ((), ()))
    kc = d // pieces
    nc = d // pieces

    def pre_act(c, p):
        rows, ks = pl.ds(c * sub, sub), pl.ds(p * kc, kc)
        part = lax.dot_general(x_ref[:, ks], u_ref[rows, ks], nt, preferred_element_type=f32).T
        if p == 0:
            ht_ref[rows, :] = part
        else:
            ht_ref[rows, :] += part

    def gate(c, p):
        rp = sub // pieces
        er = pl.ds(c * sub + p * rp, rp)
        hw_ref[er, :] = (_gelu(ht_ref[er, :]) * g_ref[er, :].astype(f32)).astype(bf16)

    def mix(c, p):
        rows, ns = pl.ds(c * sub, sub), pl.ds(p * nc, nc)
        o_ref[:, ns] += lax.dot_general(hw_ref[rows, :], v_ref[rows, ns], tn, preferred_element_type=f32)

    for p in range(pieces):
        pre_act(0, p)
    for c in range(n_chain):
        for p in range(pieces):
            if c + 1 < n_chain:
                pre_act(c + 1, p)
            if c > 0:
                mix(c - 1, p)
            gate(c, p)
    for p in range(pieces):
        mix(n_chain - 1, p)


def _peer_mix(layer, x, u_tab, v_tab, gate):
    t, d = x.shape
    n_exp = u_tab.shape[1]
    tt = _pick(t, (512, 256, 128))
    et, sub = 512, 256
    tab = pl.BlockSpec((None, et, d), lambda i, j: (layer, j, 0))
    return pl.pallas_call(
        functools.partial(_peer_mix_kernel, et=et, sub=sub, pieces=4),
        out_shape=jax.ShapeDtypeStruct((t, d), f32),
        grid=(t // tt, n_exp // et),
        in_specs=[pl.BlockSpec((tt, d), lambda i, j: (i, 0)), tab, tab,
                  pl.BlockSpec((et, tt), lambda i, j: (j, i))],
        out_specs=pl.BlockSpec((tt, d), lambda i, j: (i, 0)),
        scratch_shapes=[pltpu.VMEM((et, tt), f32), pltpu.VMEM((et, tt), bf16)],
        compiler_params=_params("parallel", "arbitrary"),
        name="peer_mix",
    )(x, u_tab, v_tab, gate)


def _transpose_cast_kernel(x_ref, o_ref):
    o_ref[...] = x_ref[...].T.astype(o_ref.dtype)


def _shift_transpose_cast_kernel(a_ref, b_ref, o_ref, *, shift):
    rows = jnp.concatenate([a_ref[pl.ds(shift, a_ref.shape[0] - shift), :], b_ref[...]], axis=0)
    o_ref[...] = rows.T.astype(o_ref.dtype)


def _split_w_in(w_in, head_cols, tail0):
    depth, k, n = w_in.shape
    w_t = jnp.transpose(w_in, (0, 2, 1))
    tc = 256
    base = (tail0 // tc) * tc
    shift = tail0 - base
    n_tail = n - tail0
    assert head_cols % LANES == 0 and n_tail % tc == 0 and 0 < shift < tc and shift % SUBLANES == 0
    assert base % shift == 0 and tc % shift == 0
    head = pl.pallas_call(
        _transpose_cast_kernel,
        out_shape=jax.ShapeDtypeStruct((depth, k, head_cols), bf16),
        grid=(depth, head_cols // LANES),
        in_specs=[pl.BlockSpec((None, LANES, k), lambda l, j: (l, j, 0))],
        out_specs=pl.BlockSpec((None, k, LANES), lambda l, j: (l, 0, j)),
        compiler_params=_params("parallel", "parallel"),
        name="cast_head",
    )(w_t)
    tail = pl.pallas_call(
        functools.partial(_shift_transpose_cast_kernel, shift=shift),
        out_shape=jax.ShapeDtypeStruct((depth, k, n_tail), bf16),
        grid=(depth, n_tail // tc),
        in_specs=[pl.BlockSpec((None, tc, k), lambda l, j: (l, base // tc + j, 0)),
                  pl.BlockSpec((None, shift, k), lambda l, j: (l, (base + (j + 1) * tc) // shift, 0))],
        out_specs=pl.BlockSpec((None, k, tc), lambda l, j: (l, 0, j)),
        compiler_params=_params("parallel", "parallel"),
        name="cast_tail",
    )(w_t, w_t)
    return head, tail


def _prep_weights(w_in, ssd_dt_bias, w_ssd_out, w_pool_out, w_gmlp_out, w_conf_out, w_o,
                  peer_w_q, peer_sub_keys, peer_u, peer_v, ple_w_gate, ple_w_proj, sizes):
    cast = lambda w: w.astype(bf16)
    inner, xbc_w, n_heads = sizes[:3]
    dt_off = inner + xbc_w
    tail0 = dt_off + n_heads
    assert dt_off % LANES == 0 and n_heads <= LANES and all(s % LANES == 0 for s in sizes[3:])
    depth = w_in.shape[0]
    dt_bias = jnp.zeros((depth, 1, LANES), f32).at[:, 0, :n_heads].set(ssd_dt_bias)
    dt_mask = (jnp.arange(LANES) < n_heads).astype(f32).reshape(1, LANES)
    head, tail = _split_w_in(w_in, dt_off + LANES, tail0)
    return dict(head=head, tail=tail, dt_off=dt_off,
                dt_bias=dt_bias, dt_mask=dt_mask,
                w_ssd_out=cast(w_ssd_out), w_pool_out=cast(w_pool_out), w_gmlp_out=cast(w_gmlp_out),
                w_conf_out=cast(w_conf_out), w_o=cast(w_o), w_q=cast(peer_w_q), keys=cast(peer_sub_keys),
                u=cast(peer_u), v=cast(peer_v), w_ple_gate=cast(ple_w_gate), w_ple_proj=cast(ple_w_proj))


def _tail_rows(hist, x3, keep):
    seq_len = x3.shape[1]
    if seq_len >= keep:
        return x3[:, seq_len - keep:]
    return jnp.concatenate([hist[:, hist.shape[1] - (keep - seq_len):], x3], axis=1)


def _decoder_layer(layer, x, xb, p_b, conv_hist, h0, pool_hist, conf_hist, hist_layer, first_pos, n_batch, seq_len,
                   wts, lp, alpha):
    d_model = x.shape[1]
    inner = lp['ssd_norm_g'].shape[0]
    xbc_w = lp['ssd_conv_w'].shape[1]
    pool_w = lp['pool_scale'].shape[0]
    gmlp_w = lp['gmlp_ln_g'].shape[0]
    conf_w = lp['conf_w'].shape[1]
    head, tail = wts['head'], wts['tail']
    gmlp_off = pool_w
    conf_off = gmlp_off + 2 * gmlp_w
    gate_off = conf_off + 2 * conf_w

    ident = lambda acc: acc
    z_act = _mm("proj_z", layer, [xb], [(0, head, 0)], inner, lambda acc: _silu(acc))
    xbc_raw = _mm("proj_xbc", layer, [xb], [(0, head, inner)], xbc_w, ident)
    dt = _mm("proj_dt", layer, [xb], [(0, head, wts['dt_off'])], LANES,
             lambda acc, bias, mask: jnp.where(mask > 0.0, _softplus(acc + bias), 0.0),
             extras=[(wts['dt_bias'][layer], 'row', 0), (wts['dt_mask'], 'row', 0)])
    pool_u = _mm("proj_pool", layer, [xb], [(0, tail, 0)], pool_w, ident)
    gmlp_h = _mm("proj_gmlp", layer, [xb], [(0, tail, gmlp_off)], 2 * gmlp_w, lambda acc: _gelu(acc))
    glu = _mm("proj_glu", layer, [xb], [(0, tail, conf_off), (0, tail, conf_off + conf_w)], conf_w,
              lambda ca, cg: ca * _sigmoid(cg))
    gates = _mm("proj_gates", layer, [xb], [(0, tail, gate_off)], N_BRANCH * d_model, lambda acc: _sigmoid(acc),
                out_dtype=bf16)

    xbc_act = _causal_conv(xbc_raw, conv_hist, hist_layer, lp['ssd_conv_w'], lp['ssd_conv_b'], n_batch, seq_len,
                           act='silu')
    y_ssd, h_new = _ssd(xbc_act, dt, z_act, lp['ssd_a_log'], lp['ssd_d'], lp['ssd_norm_g'], h0, hist_layer,
                        n_batch, seq_len)
    conv_new = _tail_rows(conv_hist[hist_layer], xbc_raw.reshape(n_batch, seq_len, xbc_w), SSD_CONV - 1)

    pooled = _multiscale_pool(pool_u, pool_hist, hist_layer, lp['pool_scale'], n_batch, seq_len, first_pos)
    pool_new = _tail_rows(pool_hist[hist_layer], pool_u.reshape(n_batch, seq_len, pool_w), pool_hist.shape[2])

    a_gmlp, v_gmlp = _gmlp(gmlp_h, lp['gmlp_ln_g'], lp['gmlp_ln_b'], lp['gmlp_ws'], lp['gmlp_bs'], seq_len)

    cconv = _causal_conv(glu, conf_hist, hist_layer, lp['conf_w'], lp['conf_b'], n_batch, seq_len, act=None)
    a_conf, = _layer_norm(cconv, lp['conf_ln_g'], lp['conf_ln_b'], [bf16], act='silu')
    conf_new = _tail_rows(conf_hist[hist_layer], glu.reshape(n_batch, seq_len, conf_w), CONF_CONV - 1)

    merged = _merge(layer, y_ssd, pooled, a_gmlp, a_conf, wts['w_ssd_out'], wts['w_pool_out'], wts['w_gmlp_out'],
                    wts['w_conf_out'], gates, d_model)
    pre1 = _mm("out_proj", layer, [merged], [(0, wts['w_o'], 0)], d_model, lambda acc, res: alpha * res + acc,
               extras=[(x, 'tile', 0)])
    x1, x1b = _layer_norm(pre1, lp['ln_mix_g'], lp['ln_mix_b'], [f32, bf16])

    q = _mm("peer_q", layer, [x1b], [(0, wts['w_q'], 0)], wts['w_q'].shape[2], ident, out_dtype=bf16)
    gate = _peer_gate(layer, q, wts['keys'])
    ffn = _peer_mix(layer, x1b, wts['u'], wts['v'], gate)
    pre2 = _mm("ple_ffn", layer, [x1b, p_b], [(0, wts['w_ple_gate'], 0), (1, wts['w_ple_proj'], 0)], d_model,
               lambda g, pr, res, f: alpha * res + f + _sigmoid(g) * pr,
               extras=[(x1, 'tile', 0), (ffn, 'tile', 0)])
    x2, x2b = _layer_norm(pre2, lp['ln_ffn_g'], lp['ln_ffn_b'], [f32, bf16])
    return x2, x2b, conv_new, h_new, pool_new, conf_new, v_gmlp


def kernel(x_prompt, x_sample, p_prompt, p_sample, state_ssd_conv, state_ssd_ssm, state_pool, state_conf, w_in, ssd_conv_w, ssd_conv_b, ssd_dt_bias, ssd_a_log, ssd_d, ssd_norm_g, w_ssd_out, pool_scale, w_pool_out, gmlp_ln_g, gmlp_ln_b, gmlp_ws, gmlp_bs, w_gmlp_out, conf_w, conf_b, conf_ln_g, conf_ln_b, w_conf_out, w_o, ln_mix_g, ln_mix_b, peer_w_q, peer_sub_keys, peer_u, peer_v, ple_w_gate, ple_w_proj, ln_ffn_g, ln_ffn_b):
    depth = w_in.shape[0]
    n_b, seq, d_model = x_prompt.shape
    n_db, dseq, _ = x_sample.shape
    alpha = (2.0 * depth) ** 0.25
    n_heads = ssd_a_log.shape[1]
    inner = ssd_norm_g.shape[1]
    xbc_w = ssd_conv_w.shape[2]
    pool_w = pool_scale.shape[1]
    gmlp_w = gmlp_ln_g.shape[1]
    conf_width = conf_w.shape[2]
    sizes = (inner, xbc_w, n_heads, pool_w, 2 * gmlp_w, 2 * conf_width, N_BRANCH * d_model)
    wts = _prep_weights(w_in, ssd_dt_bias, w_ssd_out, w_pool_out, w_gmlp_out, w_conf_out, w_o,
                        peer_w_q, peer_sub_keys, peer_u, peer_v, ple_w_gate, ple_w_proj, sizes)

    yp = x_prompt.reshape(n_b * seq, d_model)
    ys = x_sample.reshape(n_db * dseq, d_model)
    ypb, ysb = yp.astype(bf16), ys.astype(bf16)
    zeros = lambda h, c: jnp.zeros((1, n_b, h, c), f32)
    outs_p, outs_s = [], []
    for i in range(depth):
        lp = dict(ssd_conv_w=ssd_conv_w[i], ssd_conv_b=ssd_conv_b[i], ssd_a_log=ssd_a_log[i], ssd_d=ssd_d[i],
                  ssd_norm_g=ssd_norm_g[i], pool_scale=pool_scale[i], gmlp_ln_g=gmlp_ln_g[i], gmlp_ln_b=gmlp_ln_b[i],
                  gmlp_ws=gmlp_ws[i], gmlp_bs=gmlp_bs[i], conf_w=conf_w[i], conf_b=conf_b[i],
                  conf_ln_g=conf_ln_g[i], conf_ln_b=conf_ln_b[i], ln_mix_g=ln_mix_g[i], ln_mix_b=ln_mix_b[i],
                  ln_ffn_g=ln_ffn_g[i], ln_ffn_b=ln_ffn_b[i])
        yp, ypb, *st_p = _decoder_layer(
            i, yp, ypb, p_prompt[i].reshape(n_b * seq, -1).astype(bf16),
            zeros(SSD_CONV - 1, xbc_w), None, zeros(max(POOL_WINDOWS) - 1, pool_w), zeros(CONF_CONV - 1, conf_width),
            0, 0, n_b, seq, wts, lp, alpha)
        ys, ysb, *st_s = _decoder_layer(
            i, ys, ysb, p_sample[i].reshape(n_db * dseq, -1).astype(bf16),
            state_ssd_conv, state_ssd_ssm, state_pool, state_conf,
            i, PAST_LEN, n_db, dseq, wts, lp, alpha)
        outs_p.append(st_p)
        outs_s.append(st_s)
    stack = lambda outs, k: jnp.stack([o[k] for o in outs])
    return (yp.reshape(n_b, seq, d_model), ys.reshape(n_db, dseq, d_model),
            stack(outs_p, 0), stack(outs_p, 1), stack(outs_p, 2), stack(outs_p, 3),
            stack(outs_s, 0), stack(outs_s, 1), stack(outs_s, 2), stack(outs_s, 3),
            stack(outs_s, 4).reshape(depth, n_db, dseq, gmlp_w))
```

```python
import functools
import math

import jax
import jax.numpy as jnp
from jax import lax
from jax.experimental import pallas as pl
from jax.experimental.pallas import tpu as pltpu

f32 = jnp.float32
bf16 = jnp.bfloat16

V7X_VMEM_BYTES = 64 * 1024 * 1024
VMEM_LIMIT = V7X_VMEM_BYTES - 8 * 1024 * 1024
LANES = 128
SUBLANES = 8

NORM_EPS = 1e-5
SSD_HEADDIM = 64
SSD_GROUPS = 4
SSD_STATE = 128
SSD_CONV = 4
POOL_WINDOWS = (2, 4, 8, 16)
GMLP_GROUPS = 8
CONF_CONV = 31
N_BRANCH = 4
PEER_KEYS = 128
PEER_HEADS = 8
PEER_TOPK = 16
PAST_LEN = 16384
ROWS = 128
THRESHOLD_SLACK = 2.0 ** -22


def _params(*sem):
    return pltpu.CompilerParams(dimension_semantics=sem, vmem_limit_bytes=VMEM_LIMIT)


def _pick(n, prefs):
    for p in prefs:
        if n % p == 0:
            return p
    return n


def _sigmoid(x):
    return 1.0 / (1.0 + jnp.exp(-x))


def _silu(x):
    return x * _sigmoid(x)


def _gelu(x):
    return 0.5 * x * (1.0 + lax.erf(x * (1.0 / math.sqrt(2.0))))


def _softplus(x):
    return jnp.maximum(x, 0.0) + jnp.log(1.0 + jnp.exp(-jnp.abs(x)))


def _mm_kernel(*refs, n_a, w_src, n_extra, epi):
    a_refs = refs[:n_a]
    w_refs = refs[n_a:n_a + len(w_src)]
    e_refs = refs[n_a + len(w_src):n_a + len(w_src) + n_extra]
    o_ref = refs[-1]
    a_vals = [r[...] for r in a_refs]
    accs = [jnp.dot(a_vals[s], w[...], preferred_element_type=f32) for s, w in zip(w_src, w_refs)]
    o_ref[...] = epi(*accs, *[e[...] for e in e_refs]).astype(o_ref.dtype)


def _mm(name, layer, a_list, w_list, n_out, epi, extras=(), out_dtype=f32, tm=None, tn=None):
    m = a_list[0].shape[0]
    tm = tm or _pick(m, (1024, 512, 256, 128))
    wide_ok = len(w_list) == 1 and not any(kind == 'tile' for _, kind, _ in extras)
    tn = tn or _pick(n_out, ((1024,) if wide_ok else ()) + (512, 256, 128))
    in_specs, args = [], []
    for a in a_list:
        in_specs.append(pl.BlockSpec((tm, a.shape[1]), lambda j, i: (i, 0)))
        args.append(a)
    for (_, w, off) in w_list:
        ob = off // tn
        in_specs.append(pl.BlockSpec((None, w.shape[1], tn), lambda j, i, ob=ob: (layer, 0, j + ob)))
        args.append(w)
    for (e, kind, off) in extras:
        ob = off // tn
        if kind == 'tile':
            in_specs.append(pl.BlockSpec((tm, tn), lambda j, i, ob=ob: (i, j + ob)))
        else:
            in_specs.append(pl.BlockSpec((1, tn), lambda j, i, ob=ob: (0, j + ob)))
        args.append(e)
    kern = functools.partial(_mm_kernel, n_a=len(a_list), w_src=tuple(s for s, _, _ in w_list),
                             n_extra=len(extras), epi=epi)
    return pl.pallas_call(
        kern,
        out_shape=jax.ShapeDtypeStruct((m, n_out), out_dtype),
        grid=(n_out // tn, m // tm),
        in_specs=in_specs,
        out_specs=pl.BlockSpec((tm, tn), lambda j, i: (i, j)),
        compiler_params=_params("parallel", "arbitrary"),
        name=name,
    )(*args)


def _ln_kernel(x_ref, g_ref, b_ref, *o_refs, act):
    x = x_ref[...]
    mu = jnp.mean(x, axis=-1, keepdims=True)
    xc = x - mu
    var = jnp.mean(xc * xc, axis=-1, keepdims=True)
    y = xc * lax.rsqrt(var + NORM_EPS) * g_ref[...] + b_ref[...]
    if act == 'silu':
        y = _silu(y)
    for o in o_refs:
        o[...] = y.astype(o.dtype)


def _layer_norm(x, g, b, out_dtypes, act=None):
    m, n = x.shape
    tm = _pick(m, (256, 128))
    outs = pl.pallas_call(
        functools.partial(_ln_kernel, act=act),
        out_shape=[jax.ShapeDtypeStruct((m, n), d) for d in out_dtypes],
        grid=(m // tm,),
        in_specs=[pl.BlockSpec((tm, n), lambda i: (i, 0)),
                  pl.BlockSpec((1, n), lambda i: (0, 0)),
                  pl.BlockSpec((1, n), lambda i: (0, 0))],
        out_specs=[pl.BlockSpec((tm, n), lambda i: (i, 0)) for _ in out_dtypes],
        compiler_params=_params("parallel"),
        name="layer_norm",
    )(x, g.reshape(1, n), b.reshape(1, n))
    return outs


def _hist_pad(h):
    return -(-h // SUBLANES) * SUBLANES


def _fill_window(xc_ref, x_ref, h_ref, s, seq_len, hist):
    hp = _hist_pad(hist)
    r0 = pl.multiple_of(s * seq_len, SUBLANES)
    xc_ref[pl.ds(hp - hist, hist), :] = h_ref[s]
    xc_ref[pl.ds(hp, seq_len), :] = x_ref[pl.ds(r0, seq_len), :]
    return r0


def _conv_kernel(x_ref, h_ref, w_ref, b_ref, o_ref, xc_ref, *, seq_len, width, n_seq, rt, act):
    hist = width - 1
    hp = _hist_pad(hist)
    w = w_ref[...]
    bias = b_ref[...]

    def one_seq(s, carry):
        r0 = _fill_window(xc_ref, x_ref, h_ref, s, seq_len, hist)

        def tile(t, c):
            base = pl.multiple_of(t * rt, SUBLANES)
            win = xc_ref[pl.ds(base, rt + hp), :]
            acc = jnp.broadcast_to(bias, (rt, bias.shape[1]))
            rotated = {0: win}
            for k in range(width):
                base8, s = divmod(hp - hist + k, SUBLANES)
                if s not in rotated:
                    rotated[s] = pltpu.roll(win, win.shape[0] - s, axis=0)
                acc = acc + rotated[s][base8 * SUBLANES:base8 * SUBLANES + rt] * w[k:k + 1]
            if act == 'silu':
                acc = _silu(acc)
            o_ref[pl.ds(pl.multiple_of(r0 + base, SUBLANES), rt), :] = acc
            return c

        lax.fori_loop(0, seq_len // rt, tile, 0)
        return carry

    lax.fori_loop(0, n_seq, one_seq, 0)


def _seq_call(name, kernel_fn, x, hist, layer, small, n_batch, seq_len, out_dtype, ct):
    c = x.shape[1]
    hist_len = hist.shape[2]
    n_seq = 1 if seq_len >= ROWS else _pick(n_batch, (16, 8, 4, 2, 1))
    rows = n_seq * seq_len
    hp = _hist_pad(hist_len)
    in_specs = [pl.BlockSpec((rows, ct), lambda b, j: (b, j)),
                pl.BlockSpec((None, n_seq, hist_len, ct), lambda b, j: (layer, b, 0, j))]
    in_specs += [pl.BlockSpec((p.shape[0], ct), lambda b, j: (0, j)) for p in small]
    return pl.pallas_call(
        functools.partial(kernel_fn, seq_len=seq_len, n_seq=n_seq),
        out_shape=jax.ShapeDtypeStruct(x.shape, out_dtype),
        grid=(n_batch // n_seq, c // ct),
        in_specs=in_specs,
        out_specs=pl.BlockSpec((rows, ct), lambda b, j: (b, j)),
        scratch_shapes=[pltpu.VMEM((hp + seq_len, ct), f32)],
        compiler_params=_params("parallel", "parallel"),
        name=name,
    )(x, hist, *small)


def _causal_conv(x, hist, layer, w, b, n_batch, seq_len, act):
    width = w.shape[0]
    rt = _pick(seq_len, (64, 32, 16, 8))
    kern = functools.partial(_conv_kernel, width=width, rt=rt, act=act)
    ct = 256 if seq_len >= ROWS else _pick(x.shape[1], (1024, 512, 256))
    return _seq_call("causal_conv", kern, x, hist, layer, [w, b.reshape(1, -1)], n_batch, seq_len, f32, ct=ct)


def _pool_kernel(x_ref, h_ref, s_ref, o_ref, xc_ref, *, seq_len, n_seq, rt, hist, first_pos):
    hp = _hist_pad(hist)
    scale = s_ref[...]

    def one_seq(s, carry):
        r0 = _fill_window(xc_ref, x_ref, h_ref, s, seq_len, hist)
        for gi, window in enumerate(POOL_WINDOWS):
            @pl.when(pl.program_id(1) == gi)
            def _(window=window):
                def tile(t, c):
                    base = pl.multiple_of(t * rt, SUBLANES)
                    win = xc_ref[pl.ds(base, rt + hp), :]
                    tok = win[hp:hp + rt]
                    acc = tok
                    for k in range(1, window):
                        acc = acc + win[hp - k:hp - k + rt]
                    pos = first_pos + base + lax.broadcasted_iota(jnp.int32, tok.shape, 0)
                    cnt = jnp.minimum(window, pos + 1).astype(f32)
                    o_ref[pl.ds(pl.multiple_of(r0 + base, SUBLANES), rt), :] = (
                        (acc / cnt - tok) * scale).astype(o_ref.dtype)
                    return c

                lax.fori_loop(0, seq_len // rt, tile, 0)
        return carry

    lax.fori_loop(0, n_seq, one_seq, 0)


def _multiscale_pool(u, hist, layer, scale, n_batch, seq_len, first_pos):
    gw = u.shape[1] // len(POOL_WINDOWS)
    rt = _pick(seq_len, (64, 32, 16, 8))
    kern = functools.partial(_pool_kernel, rt=rt, hist=hist.shape[2], first_pos=first_pos)
    return _seq_call("pool", kern, u, hist, layer, [scale.reshape(1, -1)], n_batch, seq_len, bf16, ct=gw)


def _seg_scan(x, seg_len, reverse):
    n = x.shape[0]
    pos = lax.broadcasted_iota(jnp.int32, x.shape, 0) % seg_len
    s = 1
    while s < seg_len:
        if reverse:
            x = x + jnp.where(pos < seg_len - s, pltpu.roll(x, n - s, axis=0), 0.0)
        else:
            x = x + jnp.where(pos >= s, pltpu.roll(x, s, axis=0), 0.0)
        s *= 2
    return x


def _ssd_kernel(xs_ref, b_ref, c_ref, dt_ref, z_ref, alog_ref, dx_ref, g_ref, *rest, seq_len, n_seq, carry_state):
    if carry_state:
        y_ref, hout_ref, h_scr = rest
    else:
        h0_ref, y_ref, hout_ref = rest
    grp = pl.program_id(1)
    width = xs_ref.shape[1]
    hpg = width // SSD_HEADDIM
    rows = xs_ref.shape[0]
    nt = (((1,), (1,)), ((), ()))
    tn = (((0,), (0,)), ((), ()))

    xs = xs_ref[...]
    bm = b_ref[...].astype(bf16)
    cm = c_ref[...].astype(bf16)

    to_front = (LANES - grp * hpg) % LANES
    dt_all = dt_ref[...]
    dt = pltpu.roll(dt_all, to_front, axis=1)
    da = pltpu.roll(dt_all * (-jnp.exp(alog_ref[...])), to_front, axis=1)
    a_cum = _seg_scan(da, seq_len, reverse=False)
    to_end = _seg_scan(da, seq_len, reverse=True) - da

    lane = lax.broadcasted_iota(jnp.int32, (rows, LANES), 1)
    column = lambda v, r: jnp.broadcast_to(v[:, r:r + 1], (rows, LANES))
    heads_per_tile = LANES // SSD_HEADDIM

    def spread(v):
        tiles = []
        for k in range(width // LANES):
            t = column(v, k * heads_per_tile)
            for p in range(1, heads_per_tile):
                t = jnp.where(lane >= p * SSD_HEADDIM, column(v, k * heads_per_tile + p), t)
            tiles.append(t)
        return jnp.concatenate(tiles, axis=1)

    dtx = spread(dt)
    a_cum_x = spread(a_cum)
    to_end_x = spread(to_end)

    xdt = xs * dtx

    a_cum_t = a_cum.T
    cb = lax.dot_general(cm, bm, nt, preferred_element_type=f32)
    ri = lax.broadcasted_iota(jnp.int32, (rows, rows), 0)
    ci = lax.broadcasted_iota(jnp.int32, (rows, rows), 1)
    allowed = (ri >= ci) & ((ri // seq_len) == (ci // seq_len))
    y_tiles = []
    for k in range(width // LANES):
        xdt_tile = xdt[:, k * LANES:(k + 1) * LANES]
        acc = None
        for p in range(heads_per_tile):
            r = k * heads_per_tile + p
            a_col = column(a_cum, r)
            a_row = a_cum_t[r:r + 1, :]
            seg = jnp.exp(jnp.where(allowed, a_col - a_row, -jnp.inf))
            m_r = (cb * seg).astype(bf16)
            own = (lane >= p * SSD_HEADDIM) & (lane < (p + 1) * SSD_HEADDIM)
            part = jnp.dot(m_r, jnp.where(own, xdt_tile, 0.0).astype(bf16), preferred_element_type=f32)
            acc = part if acc is None else acc + part
        y_tiles.append(acc)
    y = jnp.concatenate(y_tiles, axis=1)

    xdt_end = (xdt * jnp.exp(to_end_x)).astype(bf16)
    total_x = a_cum_x + to_end_x
    if carry_state:
        chunk = pl.program_id(2)

        @pl.when(chunk == 0)
        def _():
            h_scr[...] = jnp.zeros_like(h_scr)

        h_t = h_scr[...]
        y_off = jnp.dot(cm, h_t.astype(bf16), preferred_element_type=f32)
        st_t = lax.dot_general(bm, xdt_end, tn, preferred_element_type=f32)
        h_new = h_t * jnp.exp(total_x[0:1, :]) + st_t
        h_scr[...] = h_new

        @pl.when(chunk == pl.num_programs(2) - 1)
        def _():
            hout_ref[...] = h_new.T.reshape(hout_ref.shape)
    else:
        h_prev = h0_ref[...].reshape(n_seq * width, SSD_STATE)
        y_all = lax.dot_general(cm, h_prev.astype(bf16), nt, preferred_element_type=f32)
        decay_t = jnp.exp(total_x).T
        row_seq = lax.broadcasted_iota(jnp.int32, (rows, 1), 0) // seq_len
        y_off = jnp.zeros((rows, width), f32)
        for s in range(n_seq):
            in_seq = row_seq == s
            y_off = y_off + jnp.where(in_seq, y_all[:, s * width:(s + 1) * width], 0.0)
            st = lax.dot_general(jnp.where(in_seq, xdt_end, jnp.zeros_like(xdt_end)), bm, tn,
                                 preferred_element_type=f32)
            decay = jnp.broadcast_to(decay_t[:, s * seq_len:s * seq_len + 1], (width, SSD_STATE))
            h_new = h_prev[s * width:(s + 1) * width] * decay + st
            hout_ref[s] = h_new.reshape(hout_ref.shape[1:])
    y = y + y_off * jnp.exp(a_cum_x)

    y = (y + xs * dx_ref[...]) * z_ref[...]
    y = y * lax.rsqrt(jnp.mean(y * y, axis=-1, keepdims=True) + NORM_EPS) * g_ref[...]
    y_ref[...] = y.astype(y_ref.dtype)


def _ssd(xbc, dt, z_act, a_log, d_skip, norm_g, h0, layer, n_batch, seq_len):
    t = xbc.shape[0]
    n_heads = a_log.shape[0]
    inner = n_heads * SSD_HEADDIM
    width = inner // SSD_GROUPS
    hpg = n_heads // SSD_GROUPS
    carry = seq_len >= ROWS
    n_seq = 1 if carry else ROWS // seq_len
    n_chunks = seq_len // ROWS if carry else 1
    n_bblk = n_batch if carry else n_batch // n_seq
    alog_pad = jnp.zeros((1, LANES), f32).at[0, :n_heads].set(a_log)
    b_blk0 = inner // SSD_STATE
    c_blk0 = b_blk0 + SSD_GROUPS
    row = lambda b, g, c: b * n_chunks + c
    state_blk = (n_seq, hpg, SSD_HEADDIM, SSD_STATE)
    in_specs = [
        pl.BlockSpec((ROWS, width), lambda b, g, c: (row(b, g, c), g)),
        pl.BlockSpec((ROWS, SSD_STATE), lambda b, g, c: (row(b, g, c), b_blk0 + g)),
        pl.BlockSpec((ROWS, SSD_STATE), lambda b, g, c: (row(b, g, c), c_blk0 + g)),
        pl.BlockSpec((ROWS, LANES), lambda b, g, c: (row(b, g, c), 0)),
        pl.BlockSpec((ROWS, width), lambda b, g, c: (row(b, g, c), g)),
        pl.BlockSpec((1, LANES), lambda b, g, c: (0, 0)),
        pl.BlockSpec((1, width), lambda b, g, c: (0, g)),
        pl.BlockSpec((1, width), lambda b, g, c: (0, g)),
    ]
    args = [xbc, xbc, xbc, dt, z_act, alog_pad, jnp.repeat(d_skip, SSD_HEADDIM).reshape(1, inner),
            norm_g.reshape(1, inner)]
    scratch = []
    if carry:
        assert h0 is None
        scratch = [pltpu.VMEM((SSD_STATE, width), f32)]
    else:
        in_specs.append(pl.BlockSpec((None,) + state_blk, lambda b, g, c: (layer, b, g, 0, 0)))
        args.append(h0)
    kern = functools.partial(_ssd_kernel, seq_len=ROWS if carry else seq_len, n_seq=n_seq, carry_state=carry)
    y, h_final = pl.pallas_call(
        kern,
        out_shape=[jax.ShapeDtypeStruct((t, inner), bf16),
                   jax.ShapeDtypeStruct((n_batch, n_heads, SSD_HEADDIM, SSD_STATE), f32)],
        grid=(n_bblk, SSD_GROUPS, n_chunks),
        in_specs=in_specs,
        out_specs=[pl.BlockSpec((ROWS, width), lambda b, g, c: (row(b, g, c), g)),
                   pl.BlockSpec(state_blk, lambda b, g, c: (b, g, 0, 0))],
        scratch_shapes=scratch,
        compiler_params=_params("parallel", "parallel", "arbitrary"),
        name="ssd",
    )(*args)
    return y, h_final


def _gmlp_kernel(h_ref, g_ref, b_ref, ws_ref, bias_ref, o_ref, v_ref):
    half = h_ref.shape[1] // 2
    u = h_ref[:, :half]
    v = h_ref[:, half:]
    mu = jnp.mean(v, axis=-1, keepdims=True)
    vc = v - mu
    var = jnp.mean(vc * vc, axis=-1, keepdims=True)
    v = vc * lax.rsqrt(var + NORM_EPS) * g_ref[...] + b_ref[...]
    v_ref[...] = v
    vb = v.astype(bf16)
    gd = half // GMLP_GROUPS
    for g in range(GMLP_GROUPS):
        cs = slice(g * gd, (g + 1) * gd)
        mixed = jnp.dot(ws_ref[g], vb[:, cs], preferred_element_type=f32) + bias_ref[:, cs]
        o_ref[:, cs] = (u[:, cs] * mixed).astype(o_ref.dtype)


def _gmlp(h, ln_g, ln_b, ws, bs, seq_len):
    t, two_w = h.shape
    half = two_w // 2
    q = ROWS if seq_len % ROWS == 0 else seq_len
    reps = ROWS // q
    tri = jnp.tril(jnp.ones((q, q), f32))
    ws_q = ws[:, :q, :q] * tri
    ws_blk = jnp.einsum('ab,gij->gaibj', jnp.eye(reps, dtype=f32), ws_q).reshape(GMLP_GROUPS, ROWS, ROWS).astype(bf16)
    bias = jnp.repeat(jnp.tile(bs[:, :q], (1, reps)).T, half // GMLP_GROUPS, axis=1)
    out, v = pl.pallas_call(
        _gmlp_kernel,
        out_shape=[jax.ShapeDtypeStruct((t, half), bf16), jax.ShapeDtypeStruct((t, half), f32)],
        grid=(t // ROWS,),
        in_specs=[pl.BlockSpec((ROWS, two_w), lambda i: (i, 0)),
                  pl.BlockSpec((1, half), lambda i: (0, 0)),
                  pl.BlockSpec((1, half), lambda i: (0, 0)),
                  pl.BlockSpec((GMLP_GROUPS, ROWS, ROWS), lambda i: (0, 0, 0)),
                  pl.BlockSpec((ROWS, half), lambda i: (0, 0))],
        out_specs=[pl.BlockSpec((ROWS, half), lambda i: (i, 0)), pl.BlockSpec((ROWS, half), lambda i: (i, 0))],
        compiler_params=_params("parallel"),
        name="gmlp",
    )(h, ln_g.reshape(1, half), ln_b.reshape(1, half), ws_blk, bias)
    return out, v


def _merge_kernel(a0, a1, a2, a3, w0, w1, w2, w3, g0, g1, g2, g3, o_ref):
    acc = g0[...].astype(f32) * jnp.dot(a0[...], w0[...], preferred_element_type=f32)
    acc += g1[...].astype(f32) * jnp.dot(a1[...], w1[...], preferred_element_type=f32)
    acc += g2[...].astype(f32) * jnp.dot(a2[...], w2[...], preferred_element_type=f32)
    acc += g3[...].astype(f32) * jnp.dot(a3[...], w3[...], preferred_element_type=f32)
    o_ref[...] = acc.astype(o_ref.dtype)


def _merge(layer, a_ssd, a_pool, a_gmlp, a_conf, w_ssd, w_pool, w_gmlp, w_conf, gates, d_model):
    t = a_ssd.shape[0]
    tm = _pick(t, (1024, 512, 256, 128))
    tn = 512
    grp_in = w_pool.shape[2]
    per_grp = w_pool.shape[3] // tn
    nb = d_model // tn
    full = lambda a: pl.BlockSpec((tm, a.shape[1]), lambda j, i: (i, 0))
    wcol = lambda w: pl.BlockSpec((None, w.shape[1], tn), lambda j, i: (layer, 0, j))
    gate = lambda k: pl.BlockSpec((tm, tn), lambda j, i, k=k: (i, k * nb + j))
    return pl.pallas_call(
        _merge_kernel,
        out_shape=jax.ShapeDtypeStruct((t, d_model), bf16),
        grid=(nb, t // tm),
        in_specs=[full(a_ssd),
                  pl.BlockSpec((tm, grp_in), lambda j, i: (i, j // per_grp)),
                  full(a_gmlp), full(a_conf),
                  wcol(w_ssd),
                  pl.BlockSpec((None, None, grp_in, tn), lambda j, i: (layer, j // per_grp, 0, j % per_grp)),
                  wcol(w_gmlp), wcol(w_conf),
                  gate(0), gate(1), gate(2), gate(3)],
        out_specs=pl.BlockSpec((tm, tn), lambda j, i: (i, j)),
        compiler_params=_params("parallel", "arbitrary"),
        name="merge",
    )(a_ssd, a_pool, a_gmlp, a_conf, w_ssd, w_pool, w_gmlp, w_conf, gates, gates, gates, gates)


def _top_values(work, out_ref, count):
    for r in range(count):
        m = jnp.max(work, axis=0, keepdims=True)
        out_ref[pl.ds(r, 1), :] = m
        work = jnp.where(work == m, -jnp.inf, work)


def _candidate_rows(k):
    return [(i, k // (i + 1)) for i in range(k)]


def _peer_gate_kernel(q_ref, k_ref, w_ref, top1, top2, cand, thr_ref, s2_ref, a_ref, b_ref):
    k = PEER_TOPK
    cand[...] = jnp.full(cand.shape, -jnp.inf, f32)
    for h in range(PEER_HEADS):
        halves = []
        for c in range(2):
            qb = q_ref[:, (2 * h + c) * LANES:(2 * h + c + 1) * LANES]
            halves.append(lax.dot_general(k_ref[h, c], qb, (((1,), (1,)), ((), ())), preferred_element_type=f32))
        s1, s2 = halves
        _top_values(s1, top1, k)
        _top_values(s2, top2, k)
        off = 0
        for i, n_j in _candidate_rows(k):
            cand[pl.ds(off, n_j), :] = top1[pl.ds(i, 1), :] + top2[pl.ds(0, n_j), :]
            off += n_j
        cv = cand[...]
        work = cv
        for r in range(k):
            tau = jnp.max(work, axis=0, keepdims=True)
            work = jnp.where(work == tau, -jnp.inf, work)
        m1 = top1[pl.ds(0, 1), :]
        m2 = top2[pl.ds(0, 1), :]
        z = jnp.sum(jnp.where(cv >= tau, jnp.exp(cv - (m1 + m2)), 0.0), axis=0, keepdims=True)
        thr_ref[h] = (tau - s1) - THRESHOLD_SLACK * (jnp.abs(tau) + jnp.abs(s1))
        s2_ref[h] = s2
        a_ref[h] = jnp.exp(s1 - m1) / z
        b_ref[h] = jnp.exp(s2 - m2)

    def gate_rows(e1, carry):
        w = jnp.zeros((PEER_KEYS, w_ref.shape[1]), f32)
        for h in range(PEER_HEADS):
            hit = s2_ref[h] >= thr_ref[h, pl.ds(e1, 1), :]
            w = w + jnp.where(hit, a_ref[h, pl.ds(e1, 1), :] * b_ref[h], 0.0)
        w_ref[pl.ds(pl.multiple_of(e1 * PEER_KEYS, PEER_KEYS), PEER_KEYS), :] = w.astype(w_ref.dtype)
        return carry

    lax.fori_loop(0, PEER_KEYS, gate_rows, 0)


def _peer_gate(layer, q, keys):
    t = q.shape[0]
    tt = _pick(t, (256, 128))
    n_cand = sum(n for _, n in _candidate_rows(PEER_TOPK))
    per_head = pltpu.VMEM((PEER_HEADS, PEER_KEYS, tt), f32)
    return pl.pallas_call(
        _peer_gate_kernel,
        out_shape=jax.ShapeDtypeStruct((PEER_KEYS * PEER_KEYS, t), bf16),
        grid=(t // tt,),
        in_specs=[pl.BlockSpec((tt, q.shape[1]), lambda i: (i, 0)),
                  pl.BlockSpec((None,) + keys.shape[1:], lambda i: (layer, 0, 0, 0, 0))],
        out_specs=pl.BlockSpec((PEER_KEYS * PEER_KEYS, tt), lambda i: (0, i)),
        scratch_shapes=[pltpu.VMEM((PEER_TOPK, tt), f32), pltpu.VMEM((PEER_TOPK, tt), f32),
                        pltpu.VMEM((_hist_pad(n_cand), tt), f32), per_head, per_head, per_head, per_head],
        compiler_params=_params("parallel"),
        name="peer_gate",
    )(q, keys)


def _peer_mix_kernel(x_ref, u_ref, v_ref, g_ref, o_ref, ht_ref, hw_ref, *, et, sub, pieces):
    j = pl.program_id(1)

    @pl.when(j == 0)
    def _():
        o_ref[...] = jnp.zeros_like(o_ref)

    n_chain = et // sub
    tt, d = x_ref.shape
    nt = (((1,), (1,)), ((), ()))
    tn = (((0,), (0,)), ((), ()))
    kc = d // pieces
    nc = d // pieces

    def pre_act(c, p):
        rows, ks = pl.ds(c * sub, sub), pl.ds(p * kc, kc)
        part = lax.dot_general(x_ref[:, ks], u_ref[rows, ks], nt, preferred_element_type=f32).T
        if p == 0:
            ht_ref[rows, :] = part
        else:
            ht_ref[rows, :] += part

    def gate(c, p):
        rp = sub // pieces
        er = pl.ds(c * sub + p * rp, rp)
        hw_ref[er, :] = (_gelu(ht_ref[er, :]) * g_ref[er, :].astype(f32)).astype(bf16)

    def mix(c, p):
        rows, ns = pl.ds(c * sub, sub), pl.ds(p * nc, nc)
        o_ref[:, ns] += lax.dot_general(hw_ref[rows, :], v_ref[rows, ns], tn, preferred_element_type=f32)

    for p in range(pieces):
        pre_act(0, p)
    for c in range(n_chain):
        for p in range(pieces):
            if c + 1 < n_chain:
                pre_act(c + 1, p)
            if c > 0:
                mix(c - 1, p)
            gate(c, p)
    for p in range(pieces):
        mix(n_chain - 1, p)


def _peer_mix(layer, x, u_tab, v_tab, gate):
    t, d = x.shape
    n_exp = u_tab.shape[1]
    tt = _pick(t, (512, 256, 128))
    et, sub = 512, 256
    tab = pl.BlockSpec((None, et, d), lambda i, j: (layer, j, 0))
    return pl.pallas_call(
        functools.partial(_peer_mix_kernel, et=et, sub=sub, pieces=4),
        out_shape=jax.ShapeDtypeStruct((t, d), f32),
        grid=(t // tt, n_exp // et),
        in_specs=[pl.BlockSpec((tt, d), lambda i, j: (i, 0)), tab, tab,
                  pl.BlockSpec((et, tt), lambda i, j: (j, i))],
        out_specs=pl.BlockSpec((tt, d), lambda i, j: (i, 0)),
        scratch_shapes=[pltpu.VMEM((et, tt), f32), pltpu.VMEM((et, tt), bf16)],
        compiler_params=_params("parallel", "arbitrary"),
        name="peer_mix",
    )(x, u_tab, v_tab, gate)


def _transpose_cast_kernel(x_ref, o_ref):
    o_ref[...] = x_ref[...].T.astype(o_ref.dtype)


def _shift_transpose_cast_kernel(a_ref, b_ref, o_ref, *, shift):
    rows = jnp.concatenate([a_ref[pl.ds(shift, a_ref.shape[0] - shift), :], b_ref[...]], axis=0)
    o_ref[...] = rows.T.astype(o_ref.dtype)


def _split_w_in(w_in, head_cols, tail0):
    depth, k, n = w_in.shape
    w_t = jnp.transpose(w_in, (0, 2, 1))
    tc = 256
    base = (tail0 // tc) * tc
    shift = tail0 - base
    n_tail = n - tail0
    assert head_cols % LANES == 0 and n_tail % tc == 0 and 0 < shift < tc and shift % SUBLANES == 0
    assert base % shift == 0 and tc % shift == 0
    head = pl.pallas_call(
        _transpose_cast_kernel,
        out_shape=jax.ShapeDtypeStruct((depth, k, head_cols), bf16),
        grid=(depth, head_cols // LANES),
        in_specs=[pl.BlockSpec((None, LANES, k), lambda l, j: (l, j, 0))],
        out_specs=pl.BlockSpec((None, k, LANES), lambda l, j: (l, 0, j)),
        compiler_params=_params("parallel", "parallel"),
        name="cast_head",
    )(w_t)
    tail = pl.pallas_call(
        functools.partial(_shift_transpose_cast_kernel, shift=shift),
        out_shape=jax.ShapeDtypeStruct((depth, k, n_tail), bf16),
        grid=(depth, n_tail // tc),
        in_specs=[pl.BlockSpec((None, tc, k), lambda l, j: (l, base // tc + j, 0)),
                  pl.BlockSpec((None, shift, k), lambda l, j: (l, (base + (j + 1) * tc) // shift, 0))],
        out_specs=pl.BlockSpec((None, k, tc), lambda l, j: (l, 0, j)),
        compiler_params=_params("parallel", "parallel"),
        name="cast_tail",
    )(w_t, w_t)
    return head, tail


def _prep_weights(w_in, ssd_dt_bias, w_ssd_out, w_pool_out, w_gmlp_out, w_conf_out, w_o,
                  peer_w_q, peer_sub_keys, peer_u, peer_v, ple_w_gate, ple_w_proj, sizes):
    cast = lambda w: w.astype(bf16)
    inner, xbc_w, n_heads = sizes[:3]
    dt_off = inner + xbc_w
    tail0 = dt_off + n_heads
    assert dt_off % LANES == 0 and n_heads <= LANES and all(s % LANES == 0 for s in sizes[3:])
    depth = w_in.shape[0]
    dt_bias = jnp.zeros((depth, 1, LANES), f32).at[:, 0, :n_heads].set(ssd_dt_bias)
    dt_mask = (jnp.arange(LANES) < n_heads).astype(f32).reshape(1, LANES)
    head, tail = _split_w_in(w_in, dt_off + LANES, tail0)
    return dict(head=head, tail=tail, dt_off=dt_off,
                dt_bias=dt_bias, dt_mask=dt_mask,
                w_ssd_out=cast(w_ssd_out), w_pool_out=cast(w_pool_out), w_gmlp_out=cast(w_gmlp_out),
                w_conf_out=cast(w_conf_out), w_o=cast(w_o), w_q=cast(peer_w_q), keys=cast(peer_sub_keys),
                u=cast(peer_u), v=cast(peer_v), w_ple_gate=cast(ple_w_gate), w_ple_proj=cast(ple_w_proj))


def _tail_rows(hist, x3, keep):
    seq_len = x3.shape[1]
    if seq_len >= keep:
        return x3[:, seq_len - keep:]
    return jnp.concatenate([hist[:, hist.shape[1] - (keep - seq_len):], x3], axis=1)


def _decoder_layer(layer, x, xb, p_b, conv_hist, h0, pool_hist, conf_hist, hist_layer, first_pos, n_batch, seq_len,
                   wts, lp, alpha):
    d_model = x.shape[1]
    inner = lp['ssd_norm_g'].shape[0]
    xbc_w = lp['ssd_conv_w'].shape[1]
    pool_w = lp['pool_scale'].shape[0]
    gmlp_w = lp['gmlp_ln_g'].shape[0]
    conf_w = lp['conf_w'].shape[1]
    head, tail = wts['head'], wts['tail']
    gmlp_off = pool_w
    conf_off = gmlp_off + 2 * gmlp_w
    gate_off = conf_off + 2 * conf_w

    ident = lambda acc: acc
    z_act = _mm("proj_z", layer, [xb], [(0, head, 0)], inner, lambda acc: _silu(acc))
    xbc_raw = _mm("proj_xbc", layer, [xb], [(0, head, inner)], xbc_w, ident)
    dt = _mm("proj_dt", layer, [xb], [(0, head, wts['dt_off'])], LANES,
             lambda acc, bias, mask: jnp.where(mask > 0.0, _softplus(acc + bias), 0.0),
             extras=[(wts['dt_bias'][layer], 'row', 0), (wts['dt_mask'], 'row', 0)])
    pool_u = _mm("proj_pool", layer, [xb], [(0, tail, 0)], pool_w, ident)
    gmlp_h = _mm("proj_gmlp", layer, [xb], [(0, tail, gmlp_off)], 2 * gmlp_w, lambda acc: _gelu(acc))
    glu = _mm("proj_glu", layer, [xb], [(0, tail, conf_off), (0, tail, conf_off + conf_w)], conf_w,
              lambda ca, cg: ca * _sigmoid(cg))
    gates = _mm("proj_gates", layer, [xb], [(0, tail, gate_off)], N_BRANCH * d_model, lambda acc: _sigmoid(acc),
                out_dtype=bf16)

    xbc_act = _causal_conv(xbc_raw, conv_hist, hist_layer, lp['ssd_conv_w'], lp['ssd_conv_b'], n_batch, seq_len,
                           act='silu')
    y_ssd, h_new = _ssd(xbc_act, dt, z_act, lp['ssd_a_log'], lp['ssd_d'], lp['ssd_norm_g'], h0, hist_layer,
                        n_batch, seq_len)
    conv_new = _tail_rows(conv_hist[hist_layer], xbc_raw.reshape(n_batch, seq_len, xbc_w), SSD_CONV - 1)

    pooled = _multiscale_pool(pool_u, pool_hist, hist_layer, lp['pool_scale'], n_batch, seq_len, first_pos)
    pool_new = _tail_rows(pool_hist[hist_layer], pool_u.reshape(n_batch, seq_len, pool_w), pool_hist.shape[2])

    a_gmlp, v_gmlp = _gmlp(gmlp_h, lp['gmlp_ln_g'], lp['gmlp_ln_b'], lp['gmlp_ws'], lp['gmlp_bs'], seq_len)

    cconv = _causal_conv(glu, conf_hist, hist_layer, lp['conf_w'], lp['conf_b'], n_batch, seq_len, act=None)
    a_conf, = _layer_norm(cconv, lp['conf_ln_g'], lp['conf_ln_b'], [bf16], act='silu')
    conf_new = _tail_rows(conf_hist[hist_layer], glu.reshape(n_batch, seq_len, conf_w), CONF_CONV - 1)

    merged = _merge(layer, y_ssd, pooled, a_gmlp, a_conf, wts['w_ssd_out'], wts['w_pool_out'], wts['w_gmlp_out'],
                    wts['w_conf_out'], gates, d_model)
    pre1 = _mm("out_proj", layer, [merged], [(0, wts['w_o'], 0)], d_model, lambda acc, res: alpha * res + acc,
               extras=[(x, 'tile', 0)])
    x1, x1b = _layer_norm(pre1, lp['ln_mix_g'], lp['ln_mix_b'], [f32, bf16])

    q = _mm("peer_q", layer, [x1b], [(0, wts['w_q'], 0)], wts['w_q'].shape[2], ident, out_dtype=bf16)
    gate = _peer_gate(layer, q, wts['keys'])
    ffn = _peer_mix(layer, x1b, wts['u'], wts['v'], gate)
    pre2 = _mm("ple_ffn", layer, [x1b, p_b], [(0, wts['w_ple_gate'], 0), (1, wts['w_ple_proj'], 0)], d_model,
               lambda g, pr, res, f: alpha * res + f + _sigmoid(g) * pr,
               extras=[(x1, 'tile', 0), (ffn, 'tile', 0)])
    x2, x2b = _layer_norm(pre2, lp['ln_ffn_g'], lp['ln_ffn_b'], [f32, bf16])
    return x2, x2b, conv_new, h_new, pool_new, conf_new, v_gmlp


def kernel(x_prompt, x_sample, p_prompt, p_sample, state_ssd_conv, state_ssd_ssm, state_pool, state_conf, w_in, ssd_conv_w, ssd_conv_b, ssd_dt_bias, ssd_a_log, ssd_d, ssd_norm_g, w_ssd_out, pool_scale, w_pool_out, gmlp_ln_g, gmlp_ln_b, gmlp_ws, gmlp_bs, w_gmlp_out, conf_w, conf_b, conf_ln_g, conf_ln_b, w_conf_out, w_o, ln_mix_g, ln_mix_b, peer_w_q, peer_sub_keys, peer_u, peer_v, ple_w_gate, ple_w_proj, ln_ffn_g, ln_ffn_b):
    depth = w_in.shape[0]
    n_b, seq, d_model = x_prompt.shape
    n_db, dseq, _ = x_sample.shape
    alpha = (2.0 * depth) ** 0.25
    n_heads = ssd_a_log.shape[1]
    inner = ssd_norm_g.shape[1]
    xbc_w = ssd_conv_w.shape[2]
    pool_w = pool_scale.shape[1]
    gmlp_w = gmlp_ln_g.shape[1]
    conf_width = conf_w.shape[2]
    sizes = (inner, xbc_w, n_heads, pool_w, 2 * gmlp_w, 2 * conf_width, N_BRANCH * d_model)
    wts = _prep_weights(w_in, ssd_dt_bias, w_ssd_out, w_pool_out, w_gmlp_out, w_conf_out, w_o,
                        peer_w_q, peer_sub_keys, peer_u, peer_v, ple_w_gate, ple_w_proj, sizes)

    yp = x_prompt.reshape(n_b * seq, d_model)
    ys = x_sample.reshape(n_db * dseq, d_model)
    ypb, ysb = yp.astype(bf16), ys.astype(bf16)
    zeros = lambda h, c: jnp.zeros((1, n_b, h, c), f32)
    outs_p, outs_s = [], []
    for i in range(depth):
        lp = dict(ssd_conv_w=ssd_conv_w[i], ssd_conv_b=ssd_conv_b[i], ssd_a_log=ssd_a_log[i], ssd_d=ssd_d[i],
                  ssd_norm_g=ssd_norm_g[i], pool_scale=pool_scale[i], gmlp_ln_g=gmlp_ln_g[i], gmlp_ln_b=gmlp_ln_b[i],
                  gmlp_ws=gmlp_ws[i], gmlp_bs=gmlp_bs[i], conf_w=conf_w[i], conf_b=conf_b[i],
                  conf_ln_g=conf_ln_g[i], conf_ln_b=conf_ln_b[i], ln_mix_g=ln_mix_g[i], ln_mix_b=ln_mix_b[i],
                  ln_ffn_g=ln_ffn_g[i], ln_ffn_b=ln_ffn_b[i])
        yp, ypb, *st_p = _decoder_layer(
            i, yp, ypb, p_prompt[i].reshape(n_b * seq, -1).astype(bf16),
            zeros(SSD_CONV - 1, xbc_w), None, zeros(max(POOL_WINDOWS) - 1, pool_w), zeros(CONF_CONV - 1, conf_width),
            0, 0, n_b, seq, wts, lp, alpha)
        ys, ysb, *st_s = _decoder_layer(
            i, ys, ysb, p_sample[i].reshape(n_db * dseq, -1).astype(bf16),
            state_ssd_conv, state_ssd_ssm, state_pool, state_conf,
            i, PAST_LEN, n_db, dseq, wts, lp, alpha)
        outs_p.append(st_p)
        outs_s.append(st_s)
    stack = lambda outs, k: jnp.stack([o[k] for o in outs])
    return (yp.reshape(n_b, seq, d_model), ys.reshape(n_db, dseq, d_model),
            stack(outs_p, 0), stack(outs_p, 1), stack(outs_p, 2), stack(outs_p, 3),
            stack(outs_s, 0), stack(outs_s, 1), stack(outs_s, 2), stack(outs_s, 3),
            stack(outs_s, 4).reshape(depth, n_db, dseq, gmlp_w))
```

```python
import functools
import math

import jax
import jax.numpy as jnp
from jax import lax
from jax.experimental import pallas as pl
from jax.experimental.pallas import tpu as pltpu

f32 = jnp.float32
bf16 = jnp.bfloat16

V7X_VMEM_BYTES = 64 * 1024 * 1024
VMEM_LIMIT = V7X_VMEM_BYTES - 8 * 1024 * 1024
LANES = 128
SUBLANES = 8

NORM_EPS = 1e-5
SSD_HEADDIM = 64
SSD_GROUPS = 4
SSD_STATE = 128
SSD_CONV = 4
POOL_WINDOWS = (2, 4, 8, 16)
GMLP_GROUPS = 8
CONF_CONV = 31
N_BRANCH = 4
PEER_KEYS = 128
PEER_HEADS = 8
PEER_TOPK = 16
PAST_LEN = 16384
ROWS = 128
THRESHOLD_SLACK = 2.0 ** -22


def _params(*sem):
    return pltpu.CompilerParams(dimension_semantics=sem, vmem_limit_bytes=VMEM_LIMIT)


def _pick(n, prefs):
    for p in prefs:
        if n % p == 0:
            return p
    return n


def _sigmoid(x):
    return 1.0 / (1.0 + jnp.exp(-x))


def _silu(x):
    return x * _sigmoid(x)


def _gelu(x):
    return 0.5 * x * (1.0 + lax.erf(x * (1.0 / math.sqrt(2.0))))


def _softplus(x):
    return jnp.maximum(x, 0.0) + jnp.log(1.0 + jnp.exp(-jnp.abs(x)))


def _mm_kernel(*refs, n_a, w_src, n_extra, epi):
    a_refs = refs[:n_a]
    w_refs = refs[n_a:n_a + len(w_src)]
    e_refs = refs[n_a + len(w_src):n_a + len(w_src) + n_extra]
    o_ref = refs[-1]
    a_vals = [r[...] for r in a_refs]
    accs = [jnp.dot(a_vals[s], w[...], preferred_element_type=f32) for s, w in zip(w_src, w_refs)]
    o_ref[...] = epi(*accs, *[e[...] for e in e_refs]).astype(o_ref.dtype)


def _mm(name, layer, a_list, w_list, n_out, epi, extras=(), out_dtype=f32, tm=None, tn=None):
    m = a_list[0].shape[0]
    tm = tm or _pick(m, (1024, 512, 256, 128))
    wide_ok = len(w_list) == 1 and not any(kind == 'tile' for _, kind, _ in extras)
    tn = tn or _pick(n_out, ((1024,) if wide_ok else ()) + (512, 256, 128))
    in_specs, args = [], []
    for a in a_list:
        in_specs.append(pl.BlockSpec((tm, a.shape[1]), lambda j, i: (i, 0)))
        args.append(a)
    for (_, w, off) in w_list:
        ob = off // tn
        in_specs.append(pl.BlockSpec((None, w.shape[1], tn), lambda j, i, ob=ob: (layer, 0, j + ob)))
        args.append(w)
    for (e, kind, off) in extras:
        ob = off // tn
        if kind == 'tile':
            in_specs.append(pl.BlockSpec((tm, tn), lambda j, i, ob=ob: (i, j + ob)))
        else:
            in_specs.append(pl.BlockSpec((1, tn), lambda j, i, ob=ob: (0, j + ob)))
        args.append(e)
    kern = functools.partial(_mm_kernel, n_a=len(a_list), w_src=tuple(s for s, _, _ in w_list),
                             n_extra=len(extras), epi=epi)
    return pl.pallas_call(
        kern,
        out_shape=jax.ShapeDtypeStruct((m, n_out), out_dtype),
        grid=(n_out // tn, m // tm),
        in_specs=in_specs,
        out_specs=pl.BlockSpec((tm, tn), lambda j, i: (i, j)),
        compiler_params=_params("parallel", "arbitrary"),
        name=name,
    )(*args)


def _mm_ln_kernel(*refs, n_a, w_src, n_extra, epi):
    a_refs = refs[:n_a]
    w_refs = refs[n_a:n_a + len(w_src)]
    e_refs = refs[n_a + len(w_src):n_a + len(w_src) + n_extra]
    g_ref, b_ref, o32_ref, o16_ref, pre_ref = refs[n_a + len(w_src) + n_extra:]
    j = pl.program_id(1)
    n_chunks, _, tn = pre_ref.shape
    a_vals = [r[...] for r in a_refs]
    accs = [jnp.dot(a_vals[s], w[...], preferred_element_type=f32) for s, w in zip(w_src, w_refs)]
    pre_ref[j] = epi(*accs, *[e[...] for e in e_refs])

    @pl.when(j == n_chunks - 1)
    def _():
        n = n_chunks * tn
        total = jnp.sum(pre_ref[0], axis=-1, keepdims=True)
        for c in range(1, n_chunks):
            total = total + jnp.sum(pre_ref[c], axis=-1, keepdims=True)
        mu = total / n
        sq = jnp.sum(jnp.square(pre_ref[0] - mu), axis=-1, keepdims=True)
        for c in range(1, n_chunks):
            sq = sq + jnp.sum(jnp.square(pre_ref[c] - mu), axis=-1, keepdims=True)
        inv = lax.rsqrt(sq / n + NORM_EPS)
        for c in range(n_chunks):
            cols = slice(c * tn, (c + 1) * tn)
            y = (pre_ref[c] - mu) * inv * g_ref[:, cols] + b_ref[:, cols]
            o32_ref[:, cols] = y
            o16_ref[:, cols] = y.astype(o16_ref.dtype)


def _mm_ln(name, layer, a_list, w_list, n_out, epi, extras, ln_g, ln_b):
    m = a_list[0].shape[0]
    tm = _pick(m, (512, 256, 128))
    tn = _pick(n_out, ((512,) if len(extras) < 2 else ()) + (256, 128))
    in_specs, args = [], []
    for a in a_list:
        in_specs.append(pl.BlockSpec((tm, a.shape[1]), lambda i, j: (i, 0)))
        args.append(a)
    for (_, w, off) in w_list:
        ob = off // tn
        in_specs.append(pl.BlockSpec((None, w.shape[1], tn), lambda i, j, ob=ob: (layer, 0, j + ob)))
        args.append(w)
    for (e, kind, off) in extras:
        assert kind == 'tile'
        ob = off // tn
        in_specs.append(pl.BlockSpec((tm, tn), lambda i, j, ob=ob: (i, j + ob)))
        args.append(e)
    for v in (ln_g, ln_b):
        in_specs.append(pl.BlockSpec((1, n_out), lambda i, j: (0, 0)))
        args.append(v.reshape(1, n_out))
    kern = functools.partial(_mm_ln_kernel, n_a=len(a_list), w_src=tuple(s for s, _, _ in w_list),
                             n_extra=len(extras), epi=epi)
    return pl.pallas_call(
        kern,
        out_shape=[jax.ShapeDtypeStruct((m, n_out), f32), jax.ShapeDtypeStruct((m, n_out), bf16)],
        grid=(m // tm, n_out // tn),
        in_specs=in_specs,
        out_specs=[pl.BlockSpec((tm, n_out), lambda i, j: (i, 0)), pl.BlockSpec((tm, n_out), lambda i, j: (i, 0))],
        scratch_shapes=[pltpu.VMEM((n_out // tn, tm, tn), f32)],
        compiler_params=pltpu.CompilerParams(dimension_semantics=("parallel", "arbitrary"),
                                             vmem_limit_bytes=V7X_VMEM_BYTES - 2 * 1024 * 1024),
        name=name,
    )(*args)


def _ln_kernel(x_ref, g_ref, b_ref, *o_refs, act):
    x = x_ref[...]
    mu = jnp.mean(x, axis=-1, keepdims=True)
    xc = x - mu
    var = jnp.mean(xc * xc, axis=-1, keepdims=True)
    y = xc * lax.rsqrt(var + NORM_EPS) * g_ref[...] + b_ref[...]
    if act == 'silu':
        y = _silu(y)
    for o in o_refs:
        o[...] = y.astype(o.dtype)


def _layer_norm(x, g, b, out_dtypes, act=None):
    m, n = x.shape
    tm = _pick(m, (256, 128))
    outs = pl.pallas_call(
        functools.partial(_ln_kernel, act=act),
        out_shape=[jax.ShapeDtypeStruct((m, n), d) for d in out_dtypes],
        grid=(m // tm,),
        in_specs=[pl.BlockSpec((tm, n), lambda i: (i, 0)),
                  pl.BlockSpec((1, n), lambda i: (0, 0)),
                  pl.BlockSpec((1, n), lambda i: (0, 0))],
        out_specs=[pl.BlockSpec((tm, n), lambda i: (i, 0)) for _ in out_dtypes],
        compiler_params=_params("parallel"),
        name="layer_norm",
    )(x, g.reshape(1, n), b.reshape(1, n))
    return outs


def _hist_pad(h):
    return -(-h // SUBLANES) * SUBLANES


def _fill_window(xc_ref, x_ref, h_ref, s, seq_len, hist):
    hp = _hist_pad(hist)
    r0 = pl.multiple_of(s * seq_len, SUBLANES)
    xc_ref[pl.ds(hp - hist, hist), :] = h_ref[s]
    xc_ref[pl.ds(hp, seq_len), :] = x_ref[pl.ds(r0, seq_len), :]
    return r0


def _conv_kernel(x_ref, h_ref, w_ref, b_ref, o_ref, xc_ref, *, seq_len, width, n_seq, rt, act):
    hist = width - 1
    hp = _hist_pad(hist)
    w = w_ref[...]
    bias = b_ref[...]

    def one_seq(s, carry):
        r0 = _fill_window(xc_ref, x_ref, h_ref, s, seq_len, hist)

        def tile(t, c):
            base = pl.multiple_of(t * rt, SUBLANES)
            win = xc_ref[pl.ds(base, rt + hp), :]
            acc = jnp.broadcast_to(bias, (rt, bias.shape[1]))
            rotated = {0: win}
            for k in range(width):
                base8, s = divmod(hp - hist + k, SUBLANES)
                if s not in rotated:
                    rotated[s] = pltpu.roll(win, win.shape[0] - s, axis=0)
                acc = acc + rotated[s][base8 * SUBLANES:base8 * SUBLANES + rt] * w[k:k + 1]
            if act == 'silu':
                acc = _silu(acc)
            o_ref[pl.ds(pl.multiple_of(r0 + base, SUBLANES), rt), :] = acc
            return c

        lax.fori_loop(0, seq_len // rt, tile, 0)
        return carry

    lax.fori_loop(0, n_seq, one_seq, 0)


def _seq_call(name, kernel_fn, x, hist, layer, small, n_batch, seq_len, out_dtype, ct):
    c = x.shape[1]
    hist_len = hist.shape[2]
    n_seq = 1 if seq_len >= ROWS else _pick(n_batch, (16, 8, 4, 2, 1))
    rows = n_seq * seq_len
    hp = _hist_pad(hist_len)
    in_specs = [pl.BlockSpec((rows, ct), lambda b, j: (b, j)),
                pl.BlockSpec((None, n_seq, hist_len, ct), lambda b, j: (layer, b, 0, j))]
    in_specs += [pl.BlockSpec((p.shape[0], ct), lambda b, j: (0, j)) for p in small]
    return pl.pallas_call(
        functools.partial(kernel_fn, seq_len=seq_len, n_seq=n_seq),
        out_shape=jax.ShapeDtypeStruct(x.shape, out_dtype),
        grid=(n_batch // n_seq, c // ct),
        in_specs=in_specs,
        out_specs=pl.BlockSpec((rows, ct), lambda b, j: (b, j)),
        scratch_shapes=[pltpu.VMEM((hp + seq_len, ct), f32)],
        compiler_params=_params("parallel", "parallel"),
        name=name,
    )(x, hist, *small)


def _causal_conv(x, hist, layer, w, b, n_batch, seq_len, act):
    width = w.shape[0]
    rt = _pick(seq_len, (64, 32, 16, 8))
    kern = functools.partial(_conv_kernel, width=width, rt=rt, act=act)
    ct = 256 if seq_len >= ROWS else _pick(x.shape[1], (1024, 512, 256))
    return _seq_call("causal_conv", kern, x, hist, layer, [w, b.reshape(1, -1)], n_batch, seq_len, f32, ct=ct)


def _pool_kernel(x_ref, h_ref, s_ref, o_ref, xc_ref, *, seq_len, n_seq, rt, hist, first_pos):
    hp = _hist_pad(hist)
    scale = s_ref[...]

    def one_seq(s, carry):
        r0 = _fill_window(xc_ref, x_ref, h_ref, s, seq_len, hist)
        for gi, window in enumerate(POOL_WINDOWS):
            @pl.when(pl.program_id(1) == gi)
            def _(window=window):
                def tile(t, c):
                    base = pl.multiple_of(t * rt, SUBLANES)
                    win = xc_ref[pl.ds(base, rt + hp), :]
                    tok = win[hp:hp + rt]
                    acc = tok
                    for k in range(1, window):
                        acc = acc + win[hp - k:hp - k + rt]
                    pos = first_pos + base + lax.broadcasted_iota(jnp.int32, tok.shape, 0)
                    cnt = jnp.minimum(window, pos + 1).astype(f32)
                    o_ref[pl.ds(pl.multiple_of(r0 + base, SUBLANES), rt), :] = (
                        (acc / cnt - tok) * scale).astype(o_ref.dtype)
                    return c

                lax.fori_loop(0, seq_len // rt, tile, 0)
        return carry

    lax.fori_loop(0, n_seq, one_seq, 0)


def _multiscale_pool(u, hist, layer, scale, n_batch, seq_len, first_pos):
    gw = u.shape[1] // len(POOL_WINDOWS)
    rt = _pick(seq_len, (64, 32, 16, 8))
    kern = functools.partial(_pool_kernel, rt=rt, hist=hist.shape[2], first_pos=first_pos)
    return _seq_call("pool", kern, u, hist, layer, [scale.reshape(1, -1)], n_batch, seq_len, bf16, ct=gw)


def _seg_scan(x, seg_len, reverse):
    n = x.shape[0]
    pos = lax.broadcasted_iota(jnp.int32, x.shape, 0) % seg_len
    s = 1
    while s < seg_len:
        if reverse:
            x = x + jnp.where(pos < seg_len - s, pltpu.roll(x, n - s, axis=0), 0.0)
        else:
            x = x + jnp.where(pos >= s, pltpu.roll(x, s, axis=0), 0.0)
        s *= 2
    return x


def _ssd_kernel(xs_ref, b_ref, c_ref, dt_ref, z_ref, alog_ref, dx_ref, g_ref, *rest, seq_len, n_seq, carry_state):
    if carry_state:
        y_ref, hout_ref, h_scr = rest
    else:
        h0_ref, y_ref, hout_ref = rest
    grp = pl.program_id(1)
    width = xs_ref.shape[1]
    hpg = width // SSD_HEADDIM
    rows = xs_ref.shape[0]
    nt = (((1,), (1,)), ((), ()))
    tn = (((0,), (0,)), ((), ()))

    xs = xs_ref[...]
    bm = b_ref[...].astype(bf16)
    cm = c_ref[...].astype(bf16)

    to_front = (LANES - grp * hpg) % LANES
    dt_all = dt_ref[...]
    dt = pltpu.roll(dt_all, to_front, axis=1)
    da = pltpu.roll(dt_all * (-jnp.exp(alog_ref[...])), to_front, axis=1)
    a_cum = _seg_scan(da, seq_len, reverse=False)
    to_end = _seg_scan(da, seq_len, reverse=True) - da

    lane = lax.broadcasted_iota(jnp.int32, (rows, LANES), 1)
    column = lambda v, r: jnp.broadcast_to(v[:, r:r + 1], (rows, LANES))
    heads_per_tile = LANES // SSD_HEADDIM

    def spread(v):
        tiles = []
        for k in range(width // LANES):
            t = column(v, k * heads_per_tile)
            for p in range(1, heads_per_tile):
                t = jnp.where(lane >= p * SSD_HEADDIM, column(v, k * heads_per_tile + p), t)
            tiles.append(t)
        return jnp.concatenate(tiles, axis=1)

    dtx = spread(dt)
    a_cum_x = spread(a_cum)
    to_end_x = spread(to_end)

    xdt = xs * dtx

    a_cum_t = a_cum.T
    cb = lax.dot_general(cm, bm, nt, preferred_element_type=f32)
    ri = lax.broadcasted_iota(jnp.int32, (rows, rows), 0)
    ci = lax.broadcasted_iota(jnp.int32, (rows, rows), 1)
    allowed = (ri >= ci) & ((ri // seq_len) == (ci // seq_len))
    y_tiles = []
    for k in range(width // LANES):
        xdt_tile = xdt[:, k * LANES:(k + 1) * LANES]
        acc = None
        for p in range(heads_per_tile):
            r = k * heads_per_tile + p
            a_col = column(a_cum, r)
            a_row = a_cum_t[r:r + 1, :]
            seg = jnp.exp(jnp.where(allowed, a_col - a_row, -jnp.inf))
            m_r = (cb * seg).astype(bf16)
            own = (lane >= p * SSD_HEADDIM) & (lane < (p + 1) * SSD_HEADDIM)
            part = jnp.dot(m_r, jnp.where(own, xdt_tile, 0.0).astype(bf16), preferred_element_type=f32)
            acc = part if acc is None else acc + part
        y_tiles.append(acc)
    y = jnp.concatenate(y_tiles, axis=1)

    xdt_end = (xdt * jnp.exp(to_end_x)).astype(bf16)
    total_x = a_cum_x + to_end_x
    if carry_state:
        chunk = pl.program_id(2)

        @pl.when(chunk == 0)
        def _():
            h_scr[...] = jnp.zeros_like(h_scr)

        h_t = h_scr[...]
        y_off = jnp.dot(cm, h_t.astype(bf16), preferred_element_type=f32)
        st_t = lax.dot_general(bm, xdt_end, tn, preferred_element_type=f32)
        h_new = h_t * jnp.exp(total_x[0:1, :]) + st_t
        h_scr[...] = h_new

        @pl.when(chunk == pl.num_programs(2) - 1)
        def _():
            hout_ref[...] = h_new.T.reshape(hout_ref.shape)
    else:
        h_prev = h0_ref[...].reshape(n_seq * width, SSD_STATE)
        y_all = lax.dot_general(cm, h_prev.astype(bf16), nt, preferred_element_type=f32)
        decay_t = jnp.exp(total_x).T
        row_seq = lax.broadcasted_iota(jnp.int32, (rows, 1), 0) // seq_len
        y_off = jnp.zeros((rows, width), f32)
        for s in range(n_seq):
            in_seq = row_seq == s
            y_off = y_off + jnp.where(in_seq, y_all[:, s * width:(s + 1) * width], 0.0)
            st = lax.dot_general(jnp.where(in_seq, xdt_end, jnp.zeros_like(xdt_end)), bm, tn,
                                 preferred_element_type=f32)
            decay = jnp.broadcast_to(decay_t[:, s * seq_len:s * seq_len + 1], (width, SSD_STATE))
            h_new = h_prev[s * width:(s + 1) * width] * decay + st
            hout_ref[s] = h_new.reshape(hout_ref.shape[1:])
    y = y + y_off * jnp.exp(a_cum_x)

    y = (y + xs * dx_ref[...]) * z_ref[...]
    y = y * lax.rsqrt(jnp.mean(y * y, axis=-1, keepdims=True) + NORM_EPS) * g_ref[...]
    y_ref[...] = y.astype(y_ref.dtype)


def _ssd(xbc, dt, z_act, a_log, d_skip, norm_g, h0, layer, n_batch, seq_len):
    t = xbc.shape[0]
    n_heads = a_log.shape[0]
    inner = n_heads * SSD_HEADDIM
    width = inner // SSD_GROUPS
    hpg = n_heads // SSD_GROUPS
    carry = seq_len >= ROWS
    n_seq = 1 if carry else ROWS // seq_len
    n_chunks = seq_len // ROWS if carry else 1
    n_bblk = n_batch if carry else n_batch // n_seq
    alog_pad = jnp.zeros((1, LANES), f32).at[0, :n_heads].set(a_log)
    b_blk0 = inner // SSD_STATE
    c_blk0 = b_blk0 + SSD_GROUPS
    row = lambda b, g, c: b * n_chunks + c
    state_blk = (n_seq, hpg, SSD_HEADDIM, SSD_STATE)
    in_specs = [
        pl.BlockSpec((ROWS, width), lambda b, g, c: (row(b, g, c), g)),
        pl.BlockSpec((ROWS, SSD_STATE), lambda b, g, c: (row(b, g, c), b_blk0 + g)),
        pl.BlockSpec((ROWS, SSD_STATE), lambda b, g, c: (row(b, g, c), c_blk0 + g)),
        pl.BlockSpec((ROWS, LANES), lambda b, g, c: (row(b, g, c), 0)),
        pl.BlockSpec((ROWS, width), lambda b, g, c: (row(b, g, c), g)),
        pl.BlockSpec((1, LANES), lambda b, g, c: (0, 0)),
        pl.BlockSpec((1, width), lambda b, g, c: (0, g)),
        pl.BlockSpec((1, width), lambda b, g, c: (0, g)),
    ]
    args = [xbc, xbc, xbc, dt, z_act, alog_pad, jnp.repeat(d_skip, SSD_HEADDIM).reshape(1, inner),
            norm_g.reshape(1, inner)]
    scratch = []
    if carry:
        assert h0 is None
        scratch = [pltpu.VMEM((SSD_STATE, width), f32)]
    else:
        in_specs.append(pl.BlockSpec((None,) + state_blk, lambda b, g, c: (layer, b, g, 0, 0)))
        args.append(h0)
    kern = functools.partial(_ssd_kernel, seq_len=ROWS if carry else seq_len, n_seq=n_seq, carry_state=carry)
    y, h_final = pl.pallas_call(
        kern,
        out_shape=[jax.ShapeDtypeStruct((t, inner), bf16),
                   jax.ShapeDtypeStruct((n_batch, n_heads, SSD_HEADDIM, SSD_STATE), f32)],
        grid=(n_bblk, SSD_GROUPS, n_chunks),
        in_specs=in_specs,
        out_specs=[pl.BlockSpec((ROWS, width), lambda b, g, c: (row(b, g, c), g)),
                   pl.BlockSpec(state_blk, lambda b, g, c: (b, g, 0, 0))],
        scratch_shapes=scratch,
        compiler_params=_params("parallel", "parallel", "arbitrary"),
        name="ssd",
    )(*args)
    return y, h_final


def _gmlp_kernel(h_ref, g_ref, b_ref, ws_ref, bias_ref, o_ref, v_ref):
    half = h_ref.shape[1] // 2
    u = h_ref[:, :half]
    v = h_ref[:, half:]
    mu = jnp.mean(v, axis=-1, keepdims=True)
    vc = v - mu
    var = jnp.mean(vc * vc, axis=-1, keepdims=True)
    v = vc * lax.rsqrt(var + NORM_EPS) * g_ref[...] + b_ref[...]
    v_ref[...] = v
    vb = v.astype(bf16)
    gd = half // GMLP_GROUPS
    for g in range(GMLP_GROUPS):
        cs = slice(g * gd, (g + 1) * gd)
        mixed = jnp.dot(ws_ref[g], vb[:, cs], preferred_element_type=f32) + bias_ref[:, cs]
        o_ref[:, cs] = (u[:, cs] * mixed).astype(o_ref.dtype)


def _gmlp(h, ln_g, ln_b, ws, bs, seq_len):
    t, two_w = h.shape
    half = two_w // 2
    q = ROWS if seq_len % ROWS == 0 else seq_len
    reps = ROWS // q
    tri = jnp.tril(jnp.ones((q, q), f32))
    ws_q = ws[:, :q, :q] * tri
    ws_blk = jnp.einsum('ab,gij->gaibj', jnp.eye(reps, dtype=f32), ws_q).reshape(GMLP_GROUPS, ROWS, ROWS).astype(bf16)
    bias = jnp.repeat(jnp.tile(bs[:, :q], (1, reps)).T, half // GMLP_GROUPS, axis=1)
    out, v = pl.pallas_call(
        _gmlp_kernel,
        out_shape=[jax.ShapeDtypeStruct((t, half), bf16), jax.ShapeDtypeStruct((t, half), f32)],
        grid=(t // ROWS,),
        in_specs=[pl.BlockSpec((ROWS, two_w), lambda i: (i, 0)),
                  pl.BlockSpec((1, half), lambda i: (0, 0)),
                  pl.BlockSpec((1, half), lambda i: (0, 0)),
                  pl.BlockSpec((GMLP_GROUPS, ROWS, ROWS), lambda i: (0, 0, 0)),
                  pl.BlockSpec((ROWS, half), lambda i: (0, 0))],
        out_specs=[pl.BlockSpec((ROWS, half), lambda i: (i, 0)), pl.BlockSpec((ROWS, half), lambda i: (i, 0))],
        compiler_params=_params("parallel"),
        name="gmlp",
    )(h, ln_g.reshape(1, half), ln_b.reshape(1, half), ws_blk, bias)
    return out, v


def _merge_kernel(a0, a1, a2, a3, w0, w1, w2, w3, g0, g1, g2, g3, o_ref):
    acc = g0[...].astype(f32) * jnp.dot(a0[...], w0[...], preferred_element_type=f32)
    acc += g1[...].astype(f32) * jnp.dot(a1[...], w1[...], preferred_element_type=f32)
    acc += g2[...].astype(f32) * jnp.dot(a2[...], w2[...], preferred_element_type=f32)
    acc += g3[...].astype(f32) * jnp.dot(a3[...], w3[...], preferred_element_type=f32)
    o_ref[...] = acc.astype(o_ref.dtype)


def _merge(layer, a_ssd, a_pool, a_gmlp, a_conf, w_ssd, w_pool, w_gmlp, w_conf, gates, d_model):
    t = a_ssd.shape[0]
    tm = _pick(t, (1024, 512, 256, 128))
    tn = 512
    grp_in = w_pool.shape[2]
    per_grp = w_pool.shape[3] // tn
    nb = d_model // tn
    full = lambda a: pl.BlockSpec((tm, a.shape[1]), lambda j, i: (i, 0))
    wcol = lambda w: pl.BlockSpec((None, w.shape[1], tn), lambda j, i: (layer, 0, j))
    gate = lambda k: pl.BlockSpec((tm, tn), lambda j, i, k=k: (i, k * nb + j))
    return pl.pallas_call(
        _merge_kernel,
        out_shape=jax.ShapeDtypeStruct((t, d_model), bf16),
        grid=(nb, t // tm),
        in_specs=[full(a_ssd),
                  pl.BlockSpec((tm, grp_in), lambda j, i: (i, j // per_grp)),
                  full(a_gmlp), full(a_conf),
                  wcol(w_ssd),
                  pl.BlockSpec((None, None, grp_in, tn), lambda j, i: (layer, j // per_grp, 0, j % per_grp)),
                  wcol(w_gmlp), wcol(w_conf),
                  gate(0), gate(1), gate(2), gate(3)],
        out_specs=pl.BlockSpec((tm, tn), lambda j, i: (i, j)),
        compiler_params=_params("parallel", "arbitrary"),
        name="merge",
    )(a_ssd, a_pool, a_gmlp, a_conf, w_ssd, w_pool, w_gmlp, w_conf, gates, gates, gates, gates)


def _top_values(work, out_ref, count):
    for r in range(count):
        m = jnp.max(work, axis=0, keepdims=True)
        out_ref[pl.ds(r, 1), :] = m
        work = jnp.where(work == m, -jnp.inf, work)


def _candidate_rows(k):
    return [(i, k // (i + 1)) for i in range(k)]


def _peer_gate_kernel(q_ref, k_ref, w_ref, top1, top2, cand, thr_ref, s2_ref, a_ref, b_ref):
    k = PEER_TOPK
    cand[...] = jnp.full(cand.shape, -jnp.inf, f32)
    for h in range(PEER_HEADS):
        halves = []
        for c in range(2):
            qb = q_ref[:, (2 * h + c) * LANES:(2 * h + c + 1) * LANES]
            halves.append(lax.dot_general(k_ref[h, c], qb, (((1,), (1,)), ((), ())), preferred_element_type=f32))
        s1, s2 = halves
        _top_values(s1, top1, k)
        _top_values(s2, top2, k)
        off = 0
        for i, n_j in _candidate_rows(k):
            cand[pl.ds(off, n_j), :] = top1[pl.ds(i, 1), :] + top2[pl.ds(0, n_j), :]
            off += n_j
        cv = cand[...]
        work = cv
        for r in range(k):
            tau = jnp.max(work, axis=0, keepdims=True)
            work = jnp.where(work == tau, -jnp.inf, work)
        m1 = top1[pl.ds(0, 1), :]
        m2 = top2[pl.ds(0, 1), :]
        z = jnp.sum(jnp.where(cv >= tau, jnp.exp(cv - (m1 + m2)), 0.0), axis=0, keepdims=True)
        thr_ref[h] = (tau - s1) - THRESHOLD_SLACK * (jnp.abs(tau) + jnp.abs(s1))
        s2_ref[h] = s2
        a_ref[h] = jnp.exp(s1 - m1) / z
        b_ref[h] = jnp.exp(s2 - m2)

    def gate_rows(e1, carry):
        w = jnp.zeros((PEER_KEYS, w_ref.shape[1]), f32)
        for h in range(PEER_HEADS):
            hit = s2_ref[h] >= thr_ref[h, pl.ds(e1, 1), :]
            w = w + jnp.where(hit, a_ref[h, pl.ds(e1, 1), :] * b_ref[h], 0.0)
        w_ref[pl.ds(pl.multiple_of(e1 * PEER_KEYS, PEER_KEYS), PEER_KEYS), :] = w.astype(w_ref.dtype)
        return carry

    lax.fori_loop(0, PEER_KEYS, gate_rows, 0)


def _peer_gate(layer, q, keys):
    t = q.shape[0]
    tt = _pick(t, (256, 128))
    n_cand = sum(n for _, n in _candidate_rows(PEER_TOPK))
    per_head = pltpu.VMEM((PEER_HEADS, PEER_KEYS, tt), f32)
    return pl.pallas_call(
        _peer_gate_kernel,
        out_shape=jax.ShapeDtypeStruct((PEER_KEYS * PEER_KEYS, t), bf16),
        grid=(t // tt,),
        in_specs=[pl.BlockSpec((tt, q.shape[1]), lambda i: (i, 0)),
                  pl.BlockSpec((None,) + keys.shape[1:], lambda i: (layer, 0, 0, 0, 0))],
        out_specs=pl.BlockSpec((PEER_KEYS * PEER_KEYS, tt), lambda i: (0, i)),
        scratch_shapes=[pltpu.VMEM((PEER_TOPK, tt), f32), pltpu.VMEM((PEER_TOPK, tt), f32),
                        pltpu.VMEM((_hist_pad(n_cand), tt), f32), per_head, per_head, per_head, per_head],
        compiler_params=_params("parallel"),
        name="peer_gate",
    )(q, keys)


def _peer_mix_kernel(x_ref, u_ref, v_ref, g_ref, o_ref, ht_ref, hw_ref, *, et, sub, pieces):
    j = pl.program_id(1)

    @pl.when(j == 0)
    def _():
        o_ref[...] = jnp.zeros_like(o_ref)

    n_chain = et // sub
    tt, d = x_ref.shape
    nt = (((1,), (1,)), ((), ()))
    tn = (((0,), (0,)), ((), ()))
    kc = d // pieces
    nc = d // pieces

    def pre_act(c, p):
        rows, ks = pl.ds(c * sub, sub), pl.ds(p * kc, kc)
        part = lax.dot_general(x_ref[:, ks], u_ref[rows, ks], nt, preferred_element_type=f32).T
        if p == 0:
            ht_ref[rows, :] = part
        else:
            ht_ref[rows, :] += part

    def gate(c, p):
        rp = sub // pieces
        er = pl.ds(c * sub + p * rp, rp)
        hw_ref[er, :] = (_gelu(ht_ref[er, :]) * g_ref[er, :].astype(f32)).astype(bf16)

    def mix(c, p):
        rows, ns = pl.ds(c * sub, sub), pl.ds(p * nc, nc)
        o_ref[:, ns] += lax.dot_general(hw_ref[rows, :], v_ref[rows, ns], tn, preferred_element_type=f32)

    for p in range(pieces):
        pre_act(0, p)
    for c in range(n_chain):
        for p in range(pieces):
            if c + 1 < n_chain:
                pre_act(c + 1, p)
            if c > 0:
                mix(c - 1, p)
            gate(c, p)
    for p in range(pieces):
        mix(n_chain - 1, p)


def _peer_mix(layer, x, u_tab, v_tab, gate):
    t, d = x.shape
    n_exp = u_tab.shape[1]
    tt = _pick(t, (512, 256, 128))
    et, sub = 512, 256
    tab = pl.BlockSpec((None, et, d), lambda i, j: (layer, j, 0))
    return pl.pallas_call(
        functools.partial(_peer_mix_kernel, et=et, sub=sub, pieces=4),
        out_shape=jax.ShapeDtypeStruct((t, d), f32),
        grid=(t // tt, n_exp // et),
        in_specs=[pl.BlockSpec((tt, d), lambda i, j: (i, 0)), tab, tab,
                  pl.BlockSpec((et, tt), lambda i, j: (j, i))],
        out_specs=pl.BlockSpec((tt, d), lambda i, j: (i, 0)),
        scratch_shapes=[pltpu.VMEM((et, tt), f32), pltpu.VMEM((et, tt), bf16)],
        compiler_params=_params("parallel", "arbitrary"),
        name="peer_mix",
    )(x, u_tab, v_tab, gate)


def _transpose_cast_kernel(x_ref, o_ref):
    o_ref[...] = x_ref[...].T.astype(o_ref.dtype)


def _shift_transpose_cast_kernel(a_ref, b_ref, o_ref, *, shift):
    rows = jnp.concatenate([a_ref[pl.ds(shift, a_ref.shape[0] - shift), :], b_ref[...]], axis=0)
    o_ref[...] = rows.T.astype(o_ref.dtype)


def _split_w_in(w_in, head_cols, tail0):
    depth, k, n = w_in.shape
    w_t = jnp.transpose(w_in, (0, 2, 1))
    tc = 256
    base = (tail0 // tc) * tc
    shift = tail0 - base
    n_tail = n - tail0
    assert head_cols % LANES == 0 and n_tail % tc == 0 and 0 < shift < tc and shift % SUBLANES == 0
    assert base % shift == 0 and tc % shift == 0
    head = pl.pallas_call(
        _transpose_cast_kernel,
        out_shape=jax.ShapeDtypeStruct((depth, k, head_cols), bf16),
        grid=(depth, head_cols // LANES),
        in_specs=[pl.BlockSpec((None, LANES, k), lambda l, j: (l, j, 0))],
        out_specs=pl.BlockSpec((None, k, LANES), lambda l, j: (l, 0, j)),
        compiler_params=_params("parallel", "parallel"),
        name="cast_head",
    )(w_t)
    tail = pl.pallas_call(
        functools.partial(_shift_transpose_cast_kernel, shift=shift),
        out_shape=jax.ShapeDtypeStruct((depth, k, n_tail), bf16),
        grid=(depth, n_tail // tc),
        in_specs=[pl.BlockSpec((None, tc, k), lambda l, j: (l, base // tc + j, 0)),
                  pl.BlockSpec((None, shift, k), lambda l, j: (l, (base + (j + 1) * tc) // shift, 0))],
        out_specs=pl.BlockSpec((None, k, tc), lambda l, j: (l, 0, j)),
        compiler_params=_params("parallel", "parallel"),
        name="cast_tail",
    )(w_t, w_t)
    return head, tail


def _prep_weights(w_in, ssd_dt_bias, w_ssd_out, w_pool_out, w_gmlp_out, w_conf_out, w_o,
                  peer_w_q, peer_sub_keys, peer_u, peer_v, ple_w_gate, ple_w_proj, sizes):
    cast = lambda w: w.astype(bf16)
    inner, xbc_w, n_heads = sizes[:3]
    dt_off = inner + xbc_w
    tail0 = dt_off + n_heads
    assert dt_off % LANES == 0 and n_heads <= LANES and all(s % LANES == 0 for s in sizes[3:])
    depth = w_in.shape[0]
    dt_bias = jnp.zeros((depth, 1, LANES), f32).at[:, 0, :n_heads].set(ssd_dt_bias)
    dt_mask = (jnp.arange(LANES) < n_heads).astype(f32).reshape(1, LANES)
    head, tail = _split_w_in(w_in, dt_off + LANES, tail0)
    return dict(head=head, tail=tail, dt_off=dt_off,
                dt_bias=dt_bias, dt_mask=dt_mask,
                w_ssd_out=cast(w_ssd_out), w_pool_out=cast(w_pool_out), w_gmlp_out=cast(w_gmlp_out),
                w_conf_out=cast(w_conf_out), w_o=cast(w_o), w_q=cast(peer_w_q), keys=cast(peer_sub_keys),
                u=cast(peer_u), v=cast(peer_v), w_ple_gate=cast(ple_w_gate), w_ple_proj=cast(ple_w_proj))


def _tail_rows(hist, x3, keep):
    seq_len = x3.shape[1]
    if seq_len >= keep:
        return x3[:, seq_len - keep:]
    return jnp.concatenate([hist[:, hist.shape[1] - (keep - seq_len):], x3], axis=1)


def _decoder_layer(layer, x, xb, p_b, conv_hist, h0, pool_hist, conf_hist, hist_layer, first_pos, n_batch, seq_len,
                   wts, lp, alpha):
    d_model = x.shape[1]
    inner = lp['ssd_norm_g'].shape[0]
    xbc_w = lp['ssd_conv_w'].shape[1]
    pool_w = lp['pool_scale'].shape[0]
    gmlp_w = lp['gmlp_ln_g'].shape[0]
    conf_w = lp['conf_w'].shape[1]
    head, tail = wts['head'], wts['tail']
    gmlp_off = pool_w
    conf_off = gmlp_off + 2 * gmlp_w
    gate_off = conf_off + 2 * conf_w

    ident = lambda acc: acc
    z_act = _mm("proj_z", layer, [xb], [(0, head, 0)], inner, lambda acc: _silu(acc))
    xbc_raw = _mm("proj_xbc", layer, [xb], [(0, head, inner)], xbc_w, ident)
    dt = _mm("proj_dt", layer, [xb], [(0, head, wts['dt_off'])], LANES,
             lambda acc, bias, mask: jnp.where(mask > 0.0, _softplus(acc + bias), 0.0),
             extras=[(wts['dt_bias'][layer], 'row', 0), (wts['dt_mask'], 'row', 0)])
    pool_u = _mm("proj_pool", layer, [xb], [(0, tail, 0)], pool_w, ident)
    gmlp_h = _mm("proj_gmlp", layer, [xb], [(0, tail, gmlp_off)], 2 * gmlp_w, lambda acc: _gelu(acc))
    glu = _mm("proj_glu", layer, [xb], [(0, tail, conf_off), (0, tail, conf_off + conf_w)], conf_w,
              lambda ca, cg: ca * _sigmoid(cg))
    gates = _mm("proj_gates", layer, [xb], [(0, tail, gate_off)], N_BRANCH * d_model, lambda acc: _sigmoid(acc),
                out_dtype=bf16)

    xbc_act = _causal_conv(xbc_raw, conv_hist, hist_layer, lp['ssd_conv_w'], lp['ssd_conv_b'], n_batch, seq_len,
                           act='silu')
    y_ssd, h_new = _ssd(xbc_act, dt, z_act, lp['ssd_a_log'], lp['ssd_d'], lp['ssd_norm_g'], h0, hist_layer,
                        n_batch, seq_len)
    conv_new = _tail_rows(conv_hist[hist_layer], xbc_raw.reshape(n_batch, seq_len, xbc_w), SSD_CONV - 1)

    pooled = _multiscale_pool(pool_u, pool_hist, hist_layer, lp['pool_scale'], n_batch, seq_len, first_pos)
    pool_new = _tail_rows(pool_hist[hist_layer], pool_u.reshape(n_batch, seq_len, pool_w), pool_hist.shape[2])

    a_gmlp, v_gmlp = _gmlp(gmlp_h, lp['gmlp_ln_g'], lp['gmlp_ln_b'], lp['gmlp_ws'], lp['gmlp_bs'], seq_len)

    cconv = _causal_conv(glu, conf_hist, hist_layer, lp['conf_w'], lp['conf_b'], n_batch, seq_len, act=None)
    a_conf, = _layer_norm(cconv, lp['conf_ln_g'], lp['conf_ln_b'], [bf16], act='silu')
    conf_new = _tail_rows(conf_hist[hist_layer], glu.reshape(n_batch, seq_len, conf_w), CONF_CONV - 1)

    merged = _merge(layer, y_ssd, pooled, a_gmlp, a_conf, wts['w_ssd_out'], wts['w_pool_out'], wts['w_gmlp_out'],
                    wts['w_conf_out'], gates, d_model)
    x1, x1b = _mm_ln("out_proj_ln", layer, [merged], [(0, wts['w_o'], 0)], d_model,
                     lambda acc, res: alpha * res + acc, [(x, 'tile', 0)], lp['ln_mix_g'], lp['ln_mix_b'])

    q = _mm("peer_q", layer, [x1b], [(0, wts['w_q'], 0)], wts['w_q'].shape[2], ident, out_dtype=bf16)
    gate = _peer_gate(layer, q, wts['keys'])
    ffn = _peer_mix(layer, x1b, wts['u'], wts['v'], gate)
    x2, x2b = _mm_ln("ple_ffn_ln", layer, [x1b, p_b], [(0, wts['w_ple_gate'], 0), (1, wts['w_ple_proj'], 0)], d_model,
                     lambda g, pr, res, f: alpha * res + f + _sigmoid(g) * pr,
                     [(x1, 'tile', 0), (ffn, 'tile', 0)], lp['ln_ffn_g'], lp['ln_ffn_b'])
    return x2, x2b, conv_new, h_new, pool_new, conf_new, v_gmlp


def kernel(x_prompt, x_sample, p_prompt, p_sample, state_ssd_conv, state_ssd_ssm, state_pool, state_conf, w_in, ssd_conv_w, ssd_conv_b, ssd_dt_bias, ssd_a_log, ssd_d, ssd_norm_g, w_ssd_out, pool_scale, w_pool_out, gmlp_ln_g, gmlp_ln_b, gmlp_ws, gmlp_bs, w_gmlp_out, conf_w, conf_b, conf_ln_g, conf_ln_b, w_conf_out, w_o, ln_mix_g, ln_mix_b, peer_w_q, peer_sub_keys, peer_u, peer_v, ple_w_gate, ple_w_proj, ln_ffn_g, ln_ffn_b):
    depth = w_in.shape[0]
    n_b, seq, d_model = x_prompt.shape
    n_db, dseq, _ = x_sample.shape
    alpha = (2.0 * depth) ** 0.25
    n_heads = ssd_a_log.shape[1]
    inner = ssd_norm_g.shape[1]
    xbc_w = ssd_conv_w.shape[2]
    pool_w = pool_scale.shape[1]
    gmlp_w = gmlp_ln_g.shape[1]
    conf_width = conf_w.shape[2]
    sizes = (inner, xbc_w, n_heads, pool_w, 2 * gmlp_w, 2 * conf_width, N_BRANCH * d_model)
    wts = _prep_weights(w_in, ssd_dt_bias, w_ssd_out, w_pool_out, w_gmlp_out, w_conf_out, w_o,
                        peer_w_q, peer_sub_keys, peer_u, peer_v, ple_w_gate, ple_w_proj, sizes)

    yp = x_prompt.reshape(n_b * seq, d_model)
    ys = x_sample.reshape(n_db * dseq, d_model)
    ypb, ysb = yp.astype(bf16), ys.astype(bf16)
    zeros = lambda h, c: jnp.zeros((1, n_b, h, c), f32)
    outs_p, outs_s = [], []
    for i in range(depth):
        lp = dict(ssd_conv_w=ssd_conv_w[i], ssd_conv_b=ssd_conv_b[i], ssd_a_log=ssd_a_log[i], ssd_d=ssd_d[i],
                  ssd_norm_g=ssd_norm_g[i], pool_scale=pool_scale[i], gmlp_ln_g=gmlp_ln_g[i], gmlp_ln_b=gmlp_ln_b[i],
                  gmlp_ws=gmlp_ws[i], gmlp_bs=gmlp_bs[i], conf_w=conf_w[i], conf_b=conf_b[i],
                  conf_ln_g=conf_ln_g[i], conf_ln_b=conf_ln_b[i], ln_mix_g=ln_mix_g[i], ln_mix_b=ln_mix_b[i],
                  ln_ffn_g=ln_ffn_g[i], ln_ffn_b=ln_ffn_b[i])
        yp, ypb, *st_p = _decoder_layer(
            i, yp, ypb, p_prompt[i].reshape(n_b * seq, -1).astype(bf16),
            zeros(SSD_CONV - 1, xbc_w), None, zeros(max(POOL_WINDOWS) - 1, pool_w), zeros(CONF_CONV - 1, conf_width),
            0, 0, n_b, seq, wts, lp, alpha)
        ys, ysb, *st_s = _decoder_layer(
            i, ys, ysb, p_sample[i].reshape(n_db * dseq, -1).astype(bf16),
            state_ssd_conv, state_ssd_ssm, state_pool, state_conf,
            i, PAST_LEN, n_db, dseq, wts, lp, alpha)
        outs_p.append(st_p)
        outs_s.append(st_s)
    stack = lambda outs, k: jnp.stack([o[k] for o in outs])
    return (yp.reshape(n_b, seq, d_model), ys.reshape(n_db, dseq, d_model),
            stack(outs_p, 0), stack(outs_p, 1), stack(outs_p, 2), stack(outs_p, 3),
            stack(outs_s, 0), stack(outs_s, 1), stack(outs_s, 2), stack(outs_s, 3),
            stack(outs_s, 4).reshape(depth, n_db, dseq, gmlp_w))
```
